```python
import math
import jax, jax.numpy as jnp
from jax import lax
import numpy as np

D_MODEL = 1024
BATCH = 8
SEQ = 2048
DEPTH = 1

CONV_WIDTH = D_MODEL
CONV_K = 3
N_HEADS = 8
HEAD_DIM = D_MODEL // (2 * N_HEADS)
V_DIM = 2 * HEAD_DIM
ATTN_QK = N_HEADS * 2 * HEAD_DIM
ATTN_V = N_HEADS * V_DIM
ROT_DIM = HEAD_DIM // 4
ROPE_THETA = 500000.0
Q_BLOCK = 128
N_KEYS = 128
N_EXPERTS = N_KEYS * N_KEYS
PK_HEADS = 8
PK_DIM = 256
PK_HALF = PK_DIM // 2
PK_TOPK = 16
PEER_CHUNK = 128
ALPHA = (2 * DEPTH) ** 0.25
BETA = (8 * DEPTH) ** -0.25
LN_EPS = 1e-5
IN_WIDTHS = (CONV_WIDTH, CONV_WIDTH, CONV_WIDTH, ATTN_QK, ATTN_QK, ATTN_V, 2 * D_MODEL)

kernel_name = "hybrid_conv_diffattn_peer_deepnorm"


def _layernorm(x, g, b):
    xf = x.astype(jnp.float32)
    mu = jnp.mean(xf, axis=-1, keepdims=True)
    var = jnp.mean(jnp.square(xf - mu), axis=-1, keepdims=True)
    y = (xf - mu) * lax.rsqrt(var + LN_EPS) * g.astype(jnp.float32) + b.astype(jnp.float32)
    return y.astype(x.dtype)


def _rmsnorm(x, g):
    xf = x.astype(jnp.float32)
    y = xf * lax.rsqrt(jnp.mean(jnp.square(xf), axis=-1, keepdims=True) + LN_EPS)
    return y * g.astype(jnp.float32)


def _partial_rope(t, cos, sin):
    half = ROT_DIM // 2
    c = cos[:, :, None, None, :]
    s = sin[:, :, None, None, :]
    r1 = t[..., :half].astype(jnp.float32)
    r2 = t[..., half:ROT_DIM].astype(jnp.float32)
    rot = jnp.concatenate([r1 * c - r2 * s, r2 * c + r1 * s], axis=-1).astype(t.dtype)
    return jnp.concatenate([rot, t[..., ROT_DIM:]], axis=-1)


def _short_conv_mixer(h, b_gate, c_gate, conv_w, conv_b):
    z = c_gate * h
    seq = z.shape[1]
    zp = jnp.pad(z, ((0, 0), (CONV_K - 1, 0), (0, 0)))
    y = conv_b + sum(conv_w[j] * zp[:, j:j + seq] for j in range(CONV_K))
    return b_gate * y


def _diff_attention(q, k, v, lam):
    bsz, seq = q.shape[0], q.shape[1]
    nblk = seq // Q_BLOCK
    scale = HEAD_DIM ** -0.5
    kf = k.astype(jnp.float32)
    vf = v.astype(jnp.float32)
    qb = q.reshape(bsz, nblk, Q_BLOCK, N_HEADS, 2, HEAD_DIM).transpose(1, 0, 2, 3, 4, 5)
    kpos = jnp.arange(seq)

    def block(args):
        q_blk, start = args
        s = jnp.einsum('bqhcd,bkhcd->bhcqk', q_blk.astype(jnp.float32), kf) * scale
        qpos = start + jnp.arange(Q_BLOCK)
        mask = kpos[None, :] <= qpos[:, None]
        s = jnp.where(mask, s, -jnp.inf)
        p = jax.nn.softmax(s, axis=-1)
        a = p[:, :, 0] - lam * p[:, :, 1]
        return jnp.einsum('bhqk,bkhe->bqhe', a, vf)

    out = lax.map(block, (qb, jnp.arange(nblk) * Q_BLOCK))
    return out.transpose(1, 0, 2, 3, 4).reshape(bsz, seq, N_HEADS, V_DIM)


def _peer(x, w_pq, sub_keys, u_tab, v_tab):
    bsz, seq, dm = x.shape
    xt = x.reshape(-1, PEER_CHUNK, dm)

    def chunk(xc):
        q = (xc @ w_pq).reshape(PEER_CHUNK, PK_HEADS, 2, PK_HALF)
        s = jnp.einsum('thcd,hcnd->thcn', q.astype(jnp.float32), sub_keys.astype(jnp.float32))
        s1, i1 = lax.top_k(s[:, :, 0], PK_TOPK)
        s2, i2 = lax.top_k(s[:, :, 1], PK_TOPK)
        cand = (s1[..., :, None] + s2[..., None, :]).reshape(PEER_CHUNK, PK_HEADS, PK_TOPK * PK_TOPK)
        top_s, top_i = lax.top_k(cand, PK_TOPK)
        e_idx = (jnp.take_along_axis(i1, top_i // PK_TOPK, axis=-1) * N_KEYS
                 + jnp.take_along_axis(i2, top_i % PK_TOPK, axis=-1))
        g = jax.nn.softmax(top_s, axis=-1)
        u_sel = u_tab[e_idx]
        v_sel = v_tab[e_idx]
        act = jax.nn.gelu(jnp.einsum('thkd,td->thk', u_sel, xc).astype(jnp.float32), approximate=False)
        w = (g * act).astype(xc.dtype)
        return jnp.einsum('thk,thkd->td', w, v_sel)

    return lax.map(chunk, xt).reshape(bsz, seq, dm)


def setup_inputs(seed: int = 0) -> dict:
    key = jax.random.key(seed)
    ks = jax.random.split(key, 24)
    f32 = jnp.float32
    D = D_MODEL
    sd = D ** -0.5
    nrm = lambda k, shape, s: jax.random.normal(k, shape, f32) * s
    x = jax.random.normal(ks[0], (BATCH, SEQ, D), f32)
    offs = jax.random.randint(ks[1], (BATCH, 1), 0, 4096, dtype=jnp.int32)
    positions = offs + jnp.arange(SEQ, dtype=jnp.int32)[None, :]
    w_in = jnp.concatenate([
        nrm(ks[2], (DEPTH, D, 3 * CONV_WIDTH + 2 * ATTN_QK), sd),
        nrm(ks[3], (DEPTH, D, ATTN_V), sd * BETA),
        nrm(ks[4], (DEPTH, D, 2 * D), sd)], axis=-1)
    return {
        "x": x,
        "positions": positions,
        "w_in": w_in,
        "conv_w": nrm(ks[5], (DEPTH, CONV_K, CONV_WIDTH), CONV_K ** -0.5),
        "conv_b": nrm(ks[6], (DEPTH, CONV_WIDTH), 0.02),
        "gate_b": nrm(ks[7], (DEPTH, 2 * D), 0.1),
        "lambda_q1": nrm(ks[8], (DEPTH, HEAD_DIM), 0.1),
        "lambda_k1": nrm(ks[9], (DEPTH, HEAD_DIM), 0.1),
        "lambda_q2": nrm(ks[10], (DEPTH, HEAD_DIM), 0.1),
        "lambda_k2": nrm(ks[11], (DEPTH, HEAD_DIM), 0.1),
        "subln_g": 1.0 + nrm(ks[12], (DEPTH, V_DIM), 0.02),
        "w_conv_out": nrm(ks[13], (DEPTH, CONV_WIDTH, D), CONV_WIDTH ** -0.5),
        "w_attn_out": nrm(ks[14], (DEPTH, ATTN_V, D), ATTN_V ** -0.5),
        "w_o": nrm(ks[15], (DEPTH, D, D), sd * BETA),
        "ln1_g": 1.0 + nrm(ks[16], (DEPTH, D), 0.02),
        "ln1_b": nrm(ks[17], (DEPTH, D), 0.02),
        "w_pq": nrm(ks[18], (DEPTH, D, PK_HEADS * PK_DIM), sd),
        "sub_keys": nrm(ks[19], (DEPTH, PK_HEADS, 2, N_KEYS, PK_HALF), PK_HALF ** -0.5),
        "u_tab": nrm(ks[20], (DEPTH, N_EXPERTS, D), sd),
        "v_tab": nrm(ks[21], (DEPTH, N_EXPERTS, D), BETA),
        "ln2_g": 1.0 + nrm(ks[22], (DEPTH, D), 0.02),
        "ln2_b": nrm(ks[23], (DEPTH, D), 0.02),
    }


def reference(x, positions, w_in, conv_w, conv_b, gate_b, lambda_q1, lambda_k1, lambda_q2, lambda_k2,
              subln_g, w_conv_out, w_attn_out, w_o, ln1_g, ln1_b, w_pq, sub_keys, u_tab, v_tab,
              ln2_g, ln2_b):
    bsz, seq, _ = x.shape
    split_points = np.cumsum(IN_WIDTHS)[:-1].tolist()
    inv_freq = ROPE_THETA ** (-jnp.arange(0, ROT_DIM, 2, dtype=jnp.float32) / ROT_DIM)
    ang = positions.astype(jnp.float32)[..., None] * inv_freq
    cos, sin = jnp.cos(ang), jnp.sin(ang)
    for l in range(DEPTH):
        proj = x @ w_in[l]
        h, bg, cg, q, k, v, gates = jnp.split(proj, split_points, axis=-1)
        y_conv = _short_conv_mixer(h, bg, cg, conv_w[l], conv_b[l])
        q = _partial_rope(q.reshape(bsz, seq, N_HEADS, 2, HEAD_DIM), cos, sin)
        k = _partial_rope(k.reshape(bsz, seq, N_HEADS, 2, HEAD_DIM), cos, sin)
        lam_init = 0.8 - 0.6 * math.exp(-0.3 * l)
        lam = (jnp.exp(jnp.sum(lambda_q1[l].astype(jnp.float32) * lambda_k1[l].astype(jnp.float32)))
               - jnp.exp(jnp.sum(lambda_q2[l].astype(jnp.float32) * lambda_k2[l].astype(jnp.float32)))
               + lam_init)
        o = _diff_attention(q, k, v.reshape(bsz, seq, N_HEADS, V_DIM), lam)
        o = _rmsnorm(o, subln_g[l]) * (1.0 - lam_init)
        y_attn = o.reshape(bsz, seq, ATTN_V).astype(x.dtype)
        g = jax.nn.sigmoid(gates + gate_b[l])
        g_conv, g_attn = g[..., :D_MODEL], g[..., D_MODEL:]
        merged = g_conv * (y_conv @ w_conv_out[l]) + g_attn * (y_attn @ w_attn_out[l])
        x = _layernorm(ALPHA * x + merged @ w_o[l], ln1_g[l], ln1_b[l])
        x = _layernorm(ALPHA * x + _peer(x, w_pq[l], sub_keys[l], u_tab[l], v_tab[l]), ln2_g[l], ln2_b[l])
    return x
```

```python
import functools
import math

import jax
import jax.numpy as jnp
from jax import lax
from jax.experimental import pallas as pl
from jax.experimental.pallas import tpu as pltpu

N_HEADS = 8
HEAD_DIM = 64
V_DIM = 2 * HEAD_DIM
ROT_DIM = HEAD_DIM // 4
ROPE_THETA = 500000.0
CONV_K = 3
N_KEYS = 128
PK_HEADS = 8
PK_TOPK = 16
LN_EPS = 1e-5
CONV_HALO = 16

VMEM_LIMIT_BYTES = 52 * 1024 * 1024

F32 = jnp.float32
BF16 = jnp.bfloat16


def _resident(shape):
    nd = len(shape)
    return pl.BlockSpec(shape, lambda *_: (0,) * nd, pipeline_mode=pl.Buffered(1))


def _layernorm(r, g, b):
    mu = jnp.mean(r, axis=-1, keepdims=True)
    c = r - mu
    var = jnp.mean(c * c, axis=-1, keepdims=True)
    return c * lax.rsqrt(var + LN_EPS) * g + b


def _qkv_rope_kernel(x_ref, pos_ref, freq_ref, w_ref, q_ref, k_ref, v_ref):
    d_model = x_ref.shape[1]
    proj = jnp.dot(x_ref[...].astype(BF16), w_ref[...], preferred_element_type=F32)
    ang = pos_ref[...].astype(F32) * freq_ref[...]
    cos = jnp.cos(ang)
    sin = jnp.sin(ang)
    d = lax.broadcasted_iota(jnp.int32, ang.shape, 1) & (HEAD_DIM - 1)
    half = ROT_DIM // 2
    lo = d < half
    hi = jnp.logical_and(d >= half, d < ROT_DIM)
    c_tab = jnp.where(d < ROT_DIM, cos, 1.0)
    s_lo = jnp.where(lo, -sin, 0.0)
    s_hi = jnp.where(hi, sin, 0.0)
    scale = HEAD_DIM ** -0.5
    for h in range(N_HEADS):
        for off, ref, mul in ((0, q_ref, scale), (d_model, k_ref, 1.0)):
            t = proj[:, off + h * V_DIM: off + (h + 1) * V_DIM]
            up = jnp.where(lo, pltpu.roll(t, V_DIM - half, 1), 0.0)
            dn = jnp.where(hi, pltpu.roll(t, half, 1), 0.0)
            r = t * c_tab + up * s_lo + dn * s_hi
            ref[:, h * V_DIM:(h + 1) * V_DIM] = (r * mul).astype(BF16)
    v_ref[...] = proj[:, 2 * d_model:].astype(BF16)


def _qkv_rope(x2, pos2, freq_row, w_qkv, tm):
    t, d = x2.shape
    out = jax.ShapeDtypeStruct((t, d), BF16)
    tok = pl.BlockSpec((tm, d), lambda i: (i, 0))
    return pl.pallas_call(
        _qkv_rope_kernel,
        grid=(t // tm,),
        in_specs=[tok, pl.BlockSpec((tm, 1), lambda i: (i, 0)), _resident(freq_row.shape),
                  _resident(w_qkv.shape)],
        out_specs=[tok, tok, tok],
        out_shape=[out, out, out],
        name="qkv_rope",
        compiler_params=pltpu.CompilerParams(dimension_semantics=("parallel",),
                                             vmem_limit_bytes=VMEM_LIMIT_BYTES),
    )(x2, pos2, freq_row, w_qkv)


def _attn_kernel(lq1_ref, lk1_ref, lq2_ref, lk2_ref, g_ref, q_ref, k_ref, v_ref, o_ref,
                 m_sc, l_sc, acc_sc, *, tq, lam_init):
    i = pl.program_id(2)
    q = q_ref[...]
    lane = lax.broadcasted_iota(jnp.int32, q.shape, 1)
    zero = jnp.zeros_like(q)
    qq = jnp.concatenate([jnp.where(lane < HEAD_DIM, q, zero),
                          jnp.where(lane >= HEAD_DIM, q, zero)], axis=0)
    m_sc[...] = jnp.full(m_sc.shape, -jnp.inf, F32)
    l_sc[...] = jnp.zeros(l_sc.shape, F32)
    acc_sc[...] = jnp.zeros(acc_sc.shape, F32)

    def step(j, masked):
        start = pl.multiple_of(j * tq, tq)
        kb = k_ref[pl.ds(start, tq), :]
        vb = v_ref[pl.ds(start, tq), :]
        s = lax.dot_general(qq, kb, (((1,), (1,)), ((), ())), preferred_element_type=F32)
        if masked:
            row = lax.broadcasted_iota(jnp.int32, s.shape, 0)
            col = lax.broadcasted_iota(jnp.int32, s.shape, 1)
            qpos = jnp.where(row >= tq, row - tq, row)
            s = jnp.where(col <= qpos, s, -jnp.inf)
        m_prev = m_sc[...]
        m_new = jnp.maximum(m_prev, jnp.max(s, axis=-1, keepdims=True))
        alpha = jnp.exp(m_prev - m_new)
        p = jnp.exp(s - m_new)
        l_sc[...] = alpha * l_sc[...] + jnp.sum(p, axis=-1, keepdims=True)
        acc_sc[...] = alpha * acc_sc[...] + jnp.dot(p.astype(BF16), vb, preferred_element_type=F32)
        m_sc[...] = m_new

    def body(j, carry):
        step(j, False)
        return carry

    lax.fori_loop(0, i, body, 0)
    step(i, True)

    lam = (jnp.exp(jnp.sum(lq1_ref[...] * lk1_ref[...], axis=-1, keepdims=True))
           - jnp.exp(jnp.sum(lq2_ref[...] * lk2_ref[...], axis=-1, keepdims=True)) + lam_init)
    o = acc_sc[...] / l_sc[...]
    a = o[:tq] - lam * o[tq:]
    ms = jnp.mean(a * a, axis=-1, keepdims=True)
    y = a * lax.rsqrt(ms + LN_EPS) * g_ref[...] * (1.0 - lam_init)
    o_ref[...] = y.astype(o_ref.dtype)


def _attention(q, k, v, lq1, lk1, lq2, lk2, subln_g, bsz, seq, tq, lam_init):
    t, d = q.shape
    nq = seq // tq
    vec = lambda a: _resident(a.shape)
    return pl.pallas_call(
        functools.partial(_attn_kernel, tq=tq, lam_init=lam_init),
        grid=(bsz, N_HEADS, nq),
        in_specs=[vec(lq1), vec(lk1), vec(lq2), vec(lk2), vec(subln_g),
                  pl.BlockSpec((tq, V_DIM), lambda b, h, i: (b * nq + i, h)),
                  pl.BlockSpec((seq, V_DIM), lambda b, h, i: (b, h)),
                  pl.BlockSpec((seq, V_DIM), lambda b, h, i: (b, h))],
        out_specs=pl.BlockSpec((tq, V_DIM), lambda b, h, i: (b * nq + i, h)),
        out_shape=jax.ShapeDtypeStruct((t, d), BF16),
        scratch_shapes=[pltpu.VMEM((2 * tq, 1), F32), pltpu.VMEM((2 * tq, 1), F32),
                        pltpu.VMEM((2 * tq, V_DIM), F32)],
        name="diff_attn",
        compiler_params=pltpu.CompilerParams(
            dimension_semantics=("parallel", "parallel", "arbitrary"),
            vmem_limit_bytes=VMEM_LIMIT_BYTES),
    )(lq1, lk1, lq2, lk2, subln_g, q, k, v)


def _mix_kernel(xp_ref, x_ref, ya_ref, w_hbc_ref, w_g_ref, w_co_ref, w_ao_ref, w_o_ref,
                cw_ref, cb_ref, gb_ref, g1_ref, b1_ref, x1_ref, x1t_ref, z_sc,
                *, tm, tiles_per_seq, alpha):
    i = pl.program_id(0)
    d = x_ref.shape[1]
    x = x_ref[...]
    xcat = jnp.concatenate([xp_ref[...], x], axis=0).astype(BF16)
    hbc = jnp.dot(xcat, w_hbc_ref[...], preferred_element_type=F32)
    z = hbc[:, :d] * hbc[:, 2 * d:]
    row = lax.broadcasted_iota(jnp.int32, z.shape, 0)
    seq_start = (i % tiles_per_seq) == 0
    z_sc[...] = jnp.where(jnp.logical_and(row < CONV_HALO, seq_start), 0.0, z)
    cw = cw_ref[...]
    y = cb_ref[...] + cw[CONV_K - 1:CONV_K] * z_sc[pl.ds(CONV_HALO, tm), :]
    for j in range(CONV_K - 1):
        y = y + cw[j:j + 1] * z_sc[pl.ds(CONV_HALO - (CONV_K - 1) + j, tm), :]
    y_conv = hbc[CONV_HALO:, d:2 * d] * y
    c_out = jnp.dot(y_conv.astype(BF16), w_co_ref[...], preferred_element_type=F32)
    a_out = jnp.dot(ya_ref[...], w_ao_ref[...], preferred_element_type=F32)
    gates = jax.nn.sigmoid(
        jnp.dot(xcat[CONV_HALO:], w_g_ref[...], preferred_element_type=F32) + gb_ref[...])
    merged = gates[:, :d] * c_out + gates[:, d:] * a_out
    r = alpha * x + jnp.dot(merged.astype(BF16), w_o_ref[...], preferred_element_type=F32)
    x1 = _layernorm(r, g1_ref[...], b1_ref[...])
    x1_ref[...] = x1
    x1t_ref[...] = x1.T.astype(BF16)


def _mix(x2, y_attn, w_hbc, w_g, w_co, w_ao, w_o, conv_w, conv_b, gate_b, ln_g, ln_b,
         tm, seq, alpha):
    t, d = x2.shape
    halo_blocks = tm // CONV_HALO
    tok = pl.BlockSpec((tm, d), lambda i: (i, 0))
    prev = pl.BlockSpec((CONV_HALO, d), lambda i: (jnp.maximum(i * halo_blocks - 1, 0), 0))
    res = lambda a: _resident(a.shape)
    return pl.pallas_call(
        functools.partial(_mix_kernel, tm=tm, tiles_per_seq=seq // tm, alpha=alpha),
        grid=(t // tm,),
        in_specs=[prev, tok, tok, res(w_hbc), res(w_g), res(w_co), res(w_ao), res(w_o),
                  res(conv_w), res(conv_b), res(gate_b), res(ln_g), res(ln_b)],
        out_specs=[tok, pl.BlockSpec((d, tm), lambda i: (0, i))],
        out_shape=[jax.ShapeDtypeStruct((t, d), F32), jax.ShapeDtypeStruct((d, t), BF16)],
        scratch_shapes=[pltpu.VMEM((CONV_HALO + tm, d), F32)],
        name="conv_merge_ln1",
        compiler_params=pltpu.CompilerParams(dimension_semantics=("parallel",),
                                             vmem_limit_bytes=VMEM_LIMIT_BYTES),
    )(x2, x2, y_attn, w_hbc, w_g, w_co, w_ao, w_o, conv_w, conv_b, gate_b, ln_g, ln_b)


def _top_values(s, k):
    out = []
    for it in range(k):
        m = jnp.max(s, axis=0, keepdims=True)
        out.append(m)
        if it + 1 < k:
            s = jnp.where(s == m, -jnp.inf, s)
    return out


def _route_kernel(x1t_ref, wpqt_ref, keys_ref, e1_ref, thr_ref, s2_ref, e2_ref):
    qt = jnp.dot(wpqt_ref[...], x1t_ref[...], preferred_element_type=F32).astype(BF16)
    inf = jnp.inf
    k = PK_TOPK
    hk = k // 2
    for h in range(PK_HEADS):
        s1 = jnp.dot(keys_ref[2 * h], qt[(2 * h) * N_KEYS:(2 * h + 1) * N_KEYS],
                     preferred_element_type=F32)
        s2 = jnp.dot(keys_ref[2 * h + 1], qt[(2 * h + 1) * N_KEYS:(2 * h + 2) * N_KEYS],
                     preferred_element_type=F32)
        m1 = _top_values(s1, k)
        m2 = _top_values(s2, k)
        m2a = jnp.concatenate(m2[:hk], axis=0)
        m2b = jnp.concatenate(m2[hk:], axis=0)
        m1b = jnp.concatenate(m1[hk:], axis=0)
        cands = [m1[0] + m2a, m1[0] + m2b] + [m1[a] + m2a for a in range(1, hk)] + [m1b + m2[0]]
        top = m1[0] + m2[0]
        work = list(cands)
        theta = top
        for it in range(k):
            mx = functools.reduce(jnp.maximum, work)
            theta = jnp.max(mx, axis=0, keepdims=True)
            if it + 1 < k:
                work = [jnp.where(w == theta, -inf, w) for w in work]
        sel = [c >= theta for c in cands]
        z = functools.reduce(
            lambda a, b: a + b,
            [jnp.sum(jnp.where(sl, jnp.exp(c - top), 0.0), axis=0, keepdims=True)
             for sl, c in zip(sel, cands)])
        tau0 = jnp.min(jnp.minimum(jnp.where(sel[0], m2a, inf), jnp.where(sel[1], m2b, inf)),
                       axis=0, keepdims=True)
        taus = [tau0] + [jnp.min(jnp.where(sel[a + 1], m2a, inf), axis=0, keepdims=True)
                         for a in range(1, hk)]
        tau_b = jnp.where(sel[hk + 1], m2[0], inf)
        thr = jnp.full(s1.shape, inf, F32)
        for a in range(hk):
            thr = jnp.where(s1 == m1[a], taus[a], thr)
        for r in range(k - hk):
            thr = jnp.where(s1 == m1[hk + r], tau_b[r:r + 1], thr)
        e1_ref[h] = jnp.exp(s1 - m1[0]) / z
        thr_ref[h] = thr
        s2_ref[h] = s2
        e2_ref[h] = jnp.exp(s2 - m2[0])


def _route(x1t, wpqt, keys, tm):
    d, t = x1t.shape
    blk = pl.BlockSpec((PK_HEADS, N_KEYS, tm), lambda i: (0, 0, i))
    out = jax.ShapeDtypeStruct((PK_HEADS, N_KEYS, t), F32)
    return pl.pallas_call(
        _route_kernel,
        grid=(t // tm,),
        in_specs=[pl.BlockSpec((d, tm), lambda i: (0, i)), _resident(wpqt.shape),
                  _resident(keys.shape)],
        out_specs=[blk, blk, blk, blk],
        out_shape=[out, out, out, out],
        name="peer_route",
        compiler_params=pltpu.CompilerParams(dimension_semantics=("parallel",),
                                             vmem_limit_bytes=VMEM_LIMIT_BYTES),
    )(x1t, wpqt, keys)


def _peer_kernel(x1t_ref, u_ref, vt_ref, e1_ref, thr_ref, s2_ref, e2_ref, x1_ref, g2_ref, b2_ref,
                 o_ref, h_sc, w_sc, acc_sc, *, te, tm, lane_chunk, alpha):
    e = pl.program_id(1)

    @pl.when(e == 0)
    def _():
        acc_sc[...] = jnp.zeros(acc_sc.shape, F32)

    h_sc[...] = jnp.dot(u_ref[...], x1t_ref[...], preferred_element_type=F32)
    inv_sqrt2 = 1.0 / math.sqrt(2.0)

    def per_first_key(ii, carry):
        rows = pl.ds(pl.multiple_of(ii * N_KEYS, N_KEYS), N_KEYS)
        for c0 in range(0, tm, lane_chunk):
            cols = slice(c0, c0 + lane_chunk)
            g = jnp.zeros((N_KEYS, lane_chunk), F32)
            for h in range(PK_HEADS):
                thr_row = thr_ref[h, pl.ds(ii, 1), cols]
                e1_row = e1_ref[h, pl.ds(ii, 1), cols]
                g = g + e1_row * jnp.where(s2_ref[h, :, cols] >= thr_row, e2_ref[h, :, cols], 0.0)
            hh = h_sc[rows, cols]
            act = 0.5 * hh * (1.0 + lax.erf(hh * inv_sqrt2))
            w_sc[rows, cols] = (g * act).astype(BF16)
        return carry

    lax.fori_loop(0, te // N_KEYS, per_first_key, 0)
    acc_sc[...] += jnp.dot(vt_ref[...], w_sc[...], preferred_element_type=F32)

    @pl.when(e == pl.num_programs(1) - 1)
    def _():
        r = alpha * x1_ref[...] + acc_sc[...].T
        o_ref[...] = _layernorm(r, g2_ref[...], b2_ref[...])


def _peer(x1, x1t, u_bf, vt_bf, e1, thr, s2, e2, ln_g, ln_b, tm, te, alpha):
    t, d = x1.shape
    n_exp = u_bf.shape[0]
    n_i = te // N_KEYS
    row_blk = pl.BlockSpec((PK_HEADS, n_i, tm), lambda i, e: (0, e, i))
    all_blk = pl.BlockSpec((PK_HEADS, N_KEYS, tm), lambda i, e: (0, 0, i))
    return pl.pallas_call(
        functools.partial(_peer_kernel, te=te, tm=tm, lane_chunk=min(tm, 256), alpha=alpha),
        grid=(t // tm, n_exp // te),
        in_specs=[pl.BlockSpec((d, tm), lambda i, e: (0, i)),
                  pl.BlockSpec((te, d), lambda i, e: (e, 0)),
                  pl.BlockSpec((d, te), lambda i, e: (0, e)),
                  row_blk, row_blk, all_blk, all_blk,
                  pl.BlockSpec((tm, d), lambda i, e: (i, 0)),
                  _resident(ln_g.shape), _resident(ln_b.shape)],
        out_specs=pl.BlockSpec((tm, d), lambda i, e: (i, 0)),
        out_shape=jax.ShapeDtypeStruct((t, d), F32),
        scratch_shapes=[pltpu.VMEM((te, tm), F32), pltpu.VMEM((te, tm), BF16),
                        pltpu.VMEM((d, tm), F32)],
        name="peer_experts",
        compiler_params=pltpu.CompilerParams(dimension_semantics=("parallel", "arbitrary"),
                                             vmem_limit_bytes=VMEM_LIMIT_BYTES),
    )(x1t, u_bf, vt_bf, e1, thr, s2, e2, x1, ln_g, ln_b)


def _rope_freq_row():
    inv_freq = ROPE_THETA ** (-jnp.arange(0, ROT_DIM, 2, dtype=F32) / ROT_DIM)
    d = jnp.arange(V_DIM) % HEAD_DIM
    return jnp.where(d < ROT_DIM, inv_freq[d % (ROT_DIM // 2)], 0.0).astype(F32)[None, :]


def kernel(x, positions, w_in, conv_w, conv_b, gate_b, lambda_q1, lambda_k1, lambda_q2, lambda_k2,
           subln_g, w_conv_out, w_attn_out, w_o, ln1_g, ln1_b, w_pq, sub_keys, u_tab, v_tab,
           ln2_g, ln2_b):
    bsz, seq, d = x.shape
    depth = w_in.shape[0]
    t = bsz * seq
    assert d == N_HEADS * V_DIM and sub_keys.shape[1:] == (PK_HEADS, 2, N_KEYS, N_KEYS)
    tm = min(512, seq)
    tq = min(256, seq)
    te = 1024
    alpha = (2 * depth) ** 0.25
    row = lambda a: a.reshape(1, -1).astype(F32)
    freq_row = _rope_freq_row()
    pos2 = positions.reshape(t, 1)
    x2 = x.reshape(t, d)
    for l in range(depth):
        lam_init = 0.8 - 0.6 * math.exp(-0.3 * l)
        w_l = w_in[l].astype(BF16)
        w_hbc, w_qkv, w_g = w_l[:, :3 * d], w_l[:, 3 * d:6 * d], w_l[:, 6 * d:]
        q, k, v = _qkv_rope(x2, pos2, freq_row, w_qkv, tm)
        y_attn = _attention(q, k, v, row(lambda_q1[l]), row(lambda_k1[l]), row(lambda_q2[l]),
                            row(lambda_k2[l]), row(subln_g[l]), bsz, seq, tq, lam_init)
        x1, x1t = _mix(x2, y_attn, w_hbc, w_g, w_conv_out[l].astype(BF16),
                       w_attn_out[l].astype(BF16), w_o[l].astype(BF16), conv_w[l].astype(F32),
                       row(conv_b[l]), row(gate_b[l]), row(ln1_g[l]), row(ln1_b[l]),
                       tm, seq, alpha)
        keys = sub_keys[l].reshape(PK_HEADS * 2, N_KEYS, N_KEYS).astype(BF16)
        e1, thr, s2, e2 = _route(x1t, w_pq[l].T.astype(BF16), keys, min(256, tm))
        x2 = _peer(x1, x1t, u_tab[l].astype(BF16), v_tab[l].T.astype(BF16), e1, thr, s2, e2,
                   row(ln2_g[l]), row(ln2_b[l]), tm, te, alpha)
    return x2.reshape(bsz, seq, d)
```

```python
import functools
import math

import jax
import jax.numpy as jnp
from jax import lax
from jax.experimental import pallas as pl
from jax.experimental.pallas import tpu as pltpu

N_HEADS = 8
HEAD_DIM = 64
V_DIM = 2 * HEAD_DIM
ROT_DIM = HEAD_DIM // 4
ROPE_THETA = 500000.0
CONV_K = 3
N_KEYS = 128
PK_HEADS = 8
PK_TOPK = 16
LN_EPS = 1e-5
CONV_HALO = 16

VMEM_LIMIT_BYTES = 52 * 1024 * 1024

F32 = jnp.float32
BF16 = jnp.bfloat16


def _resident(shape):
    nd = len(shape)
    return pl.BlockSpec(shape, lambda *_: (0,) * nd, pipeline_mode=pl.Buffered(1))


def _layernorm(r, g, b):
    mu = jnp.mean(r, axis=-1, keepdims=True)
    c = r - mu
    var = jnp.mean(c * c, axis=-1, keepdims=True)
    return c * lax.rsqrt(var + LN_EPS) * g + b


def _qkv_rope_kernel(x_ref, pos_ref, freq_ref, w_ref, q_ref, k_ref, v_ref):
    d_model = x_ref.shape[1]
    proj = jnp.dot(x_ref[...].astype(BF16), w_ref[...], preferred_element_type=F32)
    ang = pos_ref[...].astype(F32) * freq_ref[...]
    cos = jnp.cos(ang)
    sin = jnp.sin(ang)
    d = lax.broadcasted_iota(jnp.int32, ang.shape, 1) & (HEAD_DIM - 1)
    half = ROT_DIM // 2
    lo = d < half
    hi = jnp.logical_and(d >= half, d < ROT_DIM)
    c_tab = jnp.where(d < ROT_DIM, cos, 1.0)
    s_lo = jnp.where(lo, -sin, 0.0)
    s_hi = jnp.where(hi, sin, 0.0)
    scale = HEAD_DIM ** -0.5 * math.log2(math.e)
    for h in range(N_HEADS):
        for off, ref, mul in ((0, q_ref, scale), (d_model, k_ref, 1.0)):
            t = proj[:, off + h * V_DIM: off + (h + 1) * V_DIM]
            up = jnp.where(lo, pltpu.roll(t, V_DIM - half, 1), 0.0)
            dn = jnp.where(hi, pltpu.roll(t, half, 1), 0.0)
            r = t * c_tab + up * s_lo + dn * s_hi
            ref[:, h * V_DIM:(h + 1) * V_DIM] = (r * mul).astype(BF16)
    v_ref[...] = proj[:, 2 * d_model:].astype(BF16)


def _qkv_rope(x2, pos2, freq_row, w_qkv, tm):
    t, d = x2.shape
    out = jax.ShapeDtypeStruct((t, d), BF16)
    tok = pl.BlockSpec((tm, d), lambda i: (i, 0))
    return pl.pallas_call(
        _qkv_rope_kernel,
        grid=(t // tm,),
        in_specs=[tok, pl.BlockSpec((tm, 1), lambda i: (i, 0)), _resident(freq_row.shape),
                  _resident(w_qkv.shape)],
        out_specs=[tok, tok, tok],
        out_shape=[out, out, out],
        name="qkv_rope",
        compiler_params=pltpu.CompilerParams(dimension_semantics=("parallel",),
                                             vmem_limit_bytes=VMEM_LIMIT_BYTES),
    )(x2, pos2, freq_row, w_qkv)


def _attn_tile(q_ref, k_ref, v_ref, g_ref, o_ref, lam, *, n, tq, lam_init):
    q = q_ref[...]
    lane_q = lax.broadcasted_iota(jnp.int32, q.shape, 1)
    zero = jnp.zeros_like(q)
    qq = jnp.concatenate([jnp.where(lane_q < HEAD_DIM, q, zero),
                          jnp.where(lane_q >= HEAD_DIM, q, zero)], axis=0)
    s = lax.dot_general(qq, k_ref[0:n, :], (((1,), (1,)), ((), ())),
                        preferred_element_type=F32)
    n_tiles = n // V_DIM
    diag0 = (n - tq) // V_DIM
    row = lax.broadcasted_iota(jnp.int32, (2 * tq, V_DIM), 0)
    qpos = jnp.where(row >= tq, row - tq, row)
    lane = lax.broadcasted_iota(jnp.int32, (2 * tq, V_DIM), 1)
    tiles = []
    for c in range(n_tiles):
        t = s[:, c * V_DIM:(c + 1) * V_DIM]
        if c >= diag0:
            t = jnp.where(lane + (c - diag0) * V_DIM <= qpos, t, -jnp.inf)
        tiles.append(t)
    m = jnp.max(functools.reduce(jnp.maximum, tiles), axis=-1, keepdims=True)
    p = [jnp.exp2(t - m) for t in tiles]
    r = 1.0 / jnp.sum(functools.reduce(lambda a, b: a + b, p), axis=-1, keepdims=True)
    r1 = r[:tq]
    r2 = lam * r[tq:]
    a = jnp.concatenate([(pt[:tq] * r1 - pt[tq:] * r2).astype(BF16) for pt in p], axis=1)
    o = jnp.dot(a, v_ref[0:n, :], preferred_element_type=F32)
    ms = jnp.mean(o * o, axis=-1, keepdims=True)
    y = o * lax.rsqrt(ms + LN_EPS) * g_ref[...] * (1.0 - lam_init)
    o_ref[...] = y.astype(o_ref.dtype)


def _attn_kernel(lq1_ref, lk1_ref, lq2_ref, lk2_ref, g_ref, q_ref, k_ref, v_ref, o_ref,
                 *, tq, nq, lam_init):
    i = pl.program_id(2)
    lam = (jnp.exp(jnp.sum(lq1_ref[...] * lk1_ref[...], axis=-1, keepdims=True))
           - jnp.exp(jnp.sum(lq2_ref[...] * lk2_ref[...], axis=-1, keepdims=True)) + lam_init)
    for nb in range(1, nq + 1):
        @pl.when(i == nb - 1)
        def _(nb=nb):
            _attn_tile(q_ref, k_ref, v_ref, g_ref, o_ref, lam, n=nb * tq, tq=tq, lam_init=lam_init)


def _attention(q, k, v, lq1, lk1, lq2, lk2, subln_g, bsz, seq, tq, lam_init):
    t, d = q.shape
    nq = seq // tq
    vec = lambda a: _resident(a.shape)
    return pl.pallas_call(
        functools.partial(_attn_kernel, tq=tq, nq=nq, lam_init=lam_init),
        grid=(bsz, N_HEADS, nq),
        in_specs=[vec(lq1), vec(lk1), vec(lq2), vec(lk2), vec(subln_g),
                  pl.BlockSpec((tq, V_DIM), lambda b, h, i: (b * nq + i, h)),
                  pl.BlockSpec((seq, V_DIM), lambda b, h, i: (b, h)),
                  pl.BlockSpec((seq, V_DIM), lambda b, h, i: (b, h))],
        out_specs=pl.BlockSpec((tq, V_DIM), lambda b, h, i: (b * nq + i, h)),
        out_shape=jax.ShapeDtypeStruct((t, d), BF16),
        name="diff_attn",
        compiler_params=pltpu.CompilerParams(
            dimension_semantics=("parallel", "parallel", "arbitrary"),
            vmem_limit_bytes=VMEM_LIMIT_BYTES),
    )(lq1, lk1, lq2, lk2, subln_g, q, k, v)


def _mix_kernel(xp_ref, x_ref, ya_ref, w_hbc_ref, w_g_ref, w_co_ref, w_ao_ref, w_o_ref,
                cw_ref, cb_ref, gb_ref, g1_ref, b1_ref, x1_ref, x1t_ref, z_sc,
                *, tm, tiles_per_seq, alpha):
    i = pl.program_id(0)
    d = x_ref.shape[1]
    x = x_ref[...]
    xcat = jnp.concatenate([xp_ref[...], x], axis=0).astype(BF16)
    hbc = jnp.dot(xcat, w_hbc_ref[...], preferred_element_type=F32)
    z = hbc[:, :d] * hbc[:, 2 * d:]
    row = lax.broadcasted_iota(jnp.int32, z.shape, 0)
    seq_start = (i % tiles_per_seq) == 0
    z_sc[...] = jnp.where(jnp.logical_and(row < CONV_HALO, seq_start), 0.0, z)
    cw = cw_ref[...]
    y = cb_ref[...] + cw[CONV_K - 1:CONV_K] * z_sc[pl.ds(CONV_HALO, tm), :]
    for j in range(CONV_K - 1):
        y = y + cw[j:j + 1] * z_sc[pl.ds(CONV_HALO - (CONV_K - 1) + j, tm), :]
    y_conv = hbc[CONV_HALO:, d:2 * d] * y
    c_out = jnp.dot(y_conv.astype(BF16), w_co_ref[...], preferred_element_type=F32)
    a_out = jnp.dot(ya_ref[...], w_ao_ref[...], preferred_element_type=F32)
    gates = jax.nn.sigmoid(
        jnp.dot(xcat[CONV_HALO:], w_g_ref[...], preferred_element_type=F32) + gb_ref[...])
    merged = gates[:, :d] * c_out + gates[:, d:] * a_out
    r = alpha * x + jnp.dot(merged.astype(BF16), w_o_ref[...], preferred_element_type=F32)
    x1 = _layernorm(r, g1_ref[...], b1_ref[...])
    x1_ref[...] = x1
    x1t_ref[...] = x1.T.astype(BF16)


def _mix(x2, y_attn, w_hbc, w_g, w_co, w_ao, w_o, conv_w, conv_b, gate_b, ln_g, ln_b,
         tm, seq, alpha):
    t, d = x2.shape
    halo_blocks = tm // CONV_HALO
    tok = pl.BlockSpec((tm, d), lambda i: (i, 0))
    prev = pl.BlockSpec((CONV_HALO, d), lambda i: (jnp.maximum(i * halo_blocks - 1, 0), 0))
    res = lambda a: _resident(a.shape)
    return pl.pallas_call(
        functools.partial(_mix_kernel, tm=tm, tiles_per_seq=seq // tm, alpha=alpha),
        grid=(t // tm,),
        in_specs=[prev, tok, tok, res(w_hbc), res(w_g), res(w_co), res(w_ao), res(w_o),
                  res(conv_w), res(conv_b), res(gate_b), res(ln_g), res(ln_b)],
        out_specs=[tok, pl.BlockSpec((d, tm), lambda i: (0, i))],
        out_shape=[jax.ShapeDtypeStruct((t, d), F32), jax.ShapeDtypeStruct((d, t), BF16)],
        scratch_shapes=[pltpu.VMEM((CONV_HALO + tm, d), F32)],
        name="conv_merge_ln1",
        compiler_params=pltpu.CompilerParams(dimension_semantics=("parallel",),
                                             vmem_limit_bytes=VMEM_LIMIT_BYTES),
    )(x2, x2, y_attn, w_hbc, w_g, w_co, w_ao, w_o, conv_w, conv_b, gate_b, ln_g, ln_b)


def _top_values(s, k):
    out = []
    for it in range(k):
        m = jnp.max(s, axis=0, keepdims=True)
        out.append(m)
        if it + 1 < k:
            s = jnp.where(s == m, -jnp.inf, s)
    return out


def _route_kernel(x1t_ref, wpqt_ref, keys_ref, e1_ref, thr_ref, s2_ref, e2_ref):
    qt = jnp.dot(wpqt_ref[...], x1t_ref[...], preferred_element_type=F32).astype(BF16)
    inf = jnp.inf
    k = PK_TOPK
    hk = k // 2
    for h in range(PK_HEADS):
        s1 = jnp.dot(keys_ref[2 * h], qt[(2 * h) * N_KEYS:(2 * h + 1) * N_KEYS],
                     preferred_element_type=F32)
        s2 = jnp.dot(keys_ref[2 * h + 1], qt[(2 * h + 1) * N_KEYS:(2 * h + 2) * N_KEYS],
                     preferred_element_type=F32)
        m1 = _top_values(s1, k)
        m2 = _top_values(s2, k)
        m2a = jnp.concatenate(m2[:hk], axis=0)
        m2b = jnp.concatenate(m2[hk:], axis=0)
        m1b = jnp.concatenate(m1[hk:], axis=0)
        cands = [m1[0] + m2a, m1[0] + m2b] + [m1[a] + m2a for a in range(1, hk)] + [m1b + m2[0]]
        top = m1[0] + m2[0]
        work = list(cands)
        theta = top
        for it in range(k):
            mx = functools.reduce(jnp.maximum, work)
            theta = jnp.max(mx, axis=0, keepdims=True)
            if it + 1 < k:
                work = [jnp.where(w == theta, -inf, w) for w in work]
        sel = [c >= theta for c in cands]
        z = functools.reduce(
            lambda a, b: a + b,
            [jnp.sum(jnp.where(sl, jnp.exp(c - top), 0.0), axis=0, keepdims=True)
             for sl, c in zip(sel, cands)])
        tau0 = jnp.min(jnp.minimum(jnp.where(sel[0], m2a, inf), jnp.where(sel[1], m2b, inf)),
                       axis=0, keepdims=True)
        taus = [tau0] + [jnp.min(jnp.where(sel[a + 1], m2a, inf), axis=0, keepdims=True)
                         for a in range(1, hk)]
        tau_b = jnp.where(sel[hk + 1], m2[0], inf)
        thr = jnp.full(s1.shape, inf, F32)
        for a in range(hk):
            thr = jnp.where(s1 == m1[a], taus[a], thr)
        for r in range(k - hk):
            thr = jnp.where(s1 == m1[hk + r], tau_b[r:r + 1], thr)
        e1_ref[h] = jnp.exp(s1 - m1[0]) / z
        thr_ref[h] = thr
        s2_ref[h] = s2
        e2_ref[h] = jnp.exp(s2 - m2[0])


def _route(x1t, wpqt, keys, tm):
    d, t = x1t.shape
    blk = pl.BlockSpec((PK_HEADS, N_KEYS, tm), lambda i: (0, 0, i))
    out = jax.ShapeDtypeStruct((PK_HEADS, N_KEYS, t), F32)
    return pl.pallas_call(
        _route_kernel,
        grid=(t // tm,),
        in_specs=[pl.BlockSpec((d, tm), lambda i: (0, i)), _resident(wpqt.shape),
                  _resident(keys.shape)],
        out_specs=[blk, blk, blk, blk],
        out_shape=[out, out, out, out],
        name="peer_route",
        compiler_params=pltpu.CompilerParams(dimension_semantics=("parallel",),
                                             vmem_limit_bytes=VMEM_LIMIT_BYTES),
    )(x1t, wpqt, keys)


def _peer_kernel(x1t_ref, u_ref, vt_ref, e1_ref, thr_ref, s2_ref, e2_ref, x1_ref, g2_ref, b2_ref,
                 o_ref, h_sc, w_sc, acc_sc, *, te, tm, lane_chunk, alpha):
    e = pl.program_id(1)

    @pl.when(e == 0)
    def _():
        acc_sc[...] = jnp.zeros(acc_sc.shape, F32)

    h_sc[...] = jnp.dot(u_ref[...], x1t_ref[...], preferred_element_type=F32)
    inv_sqrt2 = 1.0 / math.sqrt(2.0)

    def per_first_key(ii, carry):
        rows = pl.ds(pl.multiple_of(ii * N_KEYS, N_KEYS), N_KEYS)
        for c0 in range(0, tm, lane_chunk):
            cols = slice(c0, c0 + lane_chunk)
            g = jnp.zeros((N_KEYS, lane_chunk), F32)
            for h in range(PK_HEADS):
                thr_row = thr_ref[h, pl.ds(ii, 1), cols]
                e1_row = e1_ref[h, pl.ds(ii, 1), cols]
                g = g + e1_row * jnp.where(s2_ref[h, :, cols] >= thr_row, e2_ref[h, :, cols], 0.0)
            hh = h_sc[rows, cols]
            act = 0.5 * hh * (1.0 + lax.erf(hh * inv_sqrt2))
            w_sc[rows, cols] = (g * act).astype(BF16)
        return carry

    lax.fori_loop(0, te // N_KEYS, per_first_key, 0)
    acc_sc[...] += jnp.dot(vt_ref[...], w_sc[...], preferred_element_type=F32)

    @pl.when(e == pl.num_programs(1) - 1)
    def _():
        r = alpha * x1_ref[...] + acc_sc[...].T
        o_ref[...] = _layernorm(r, g2_ref[...], b2_ref[...])


def _peer(x1, x1t, u_bf, vt_bf, e1, thr, s2, e2, ln_g, ln_b, tm, te, alpha):
    t, d = x1.shape
    n_exp = u_bf.shape[0]
    n_i = te // N_KEYS
    row_blk = pl.BlockSpec((PK_HEADS, n_i, tm), lambda i, e: (0, e, i))
    all_blk = pl.BlockSpec((PK_HEADS, N_KEYS, tm), lambda i, e: (0, 0, i))
    return pl.pallas_call(
        functools.partial(_peer_kernel, te=te, tm=tm, lane_chunk=min(tm, 256), alpha=alpha),
        grid=(t // tm, n_exp // te),
        in_specs=[pl.BlockSpec((d, tm), lambda i, e: (0, i)),
                  pl.BlockSpec((te, d), lambda i, e: (e, 0)),
                  pl.BlockSpec((d, te), lambda i, e: (0, e)),
                  row_blk, row_blk, all_blk, all_blk,
                  pl.BlockSpec((tm, d), lambda i, e: (i, 0)),
                  _resident(ln_g.shape), _resident(ln_b.shape)],
        out_specs=pl.BlockSpec((tm, d), lambda i, e: (i, 0)),
        out_shape=jax.ShapeDtypeStruct((t, d), F32),
        scratch_shapes=[pltpu.VMEM((te, tm), F32), pltpu.VMEM((te, tm), BF16),
                        pltpu.VMEM((d, tm), F32)],
        name="peer_experts",
        compiler_params=pltpu.CompilerParams(dimension_semantics=("parallel", "arbitrary"),
                                             vmem_limit_bytes=VMEM_LIMIT_BYTES),
    )(x1t, u_bf, vt_bf, e1, thr, s2, e2, x1, ln_g, ln_b)


def _rope_freq_row():
    inv_freq = ROPE_THETA ** (-jnp.arange(0, ROT_DIM, 2, dtype=F32) / ROT_DIM)
    d = jnp.arange(V_DIM) % HEAD_DIM
    return jnp.where(d < ROT_DIM, inv_freq[d % (ROT_DIM // 2)], 0.0).astype(F32)[None, :]


def kernel(x, positions, w_in, conv_w, conv_b, gate_b, lambda_q1, lambda_k1, lambda_q2, lambda_k2,
           subln_g, w_conv_out, w_attn_out, w_o, ln1_g, ln1_b, w_pq, sub_keys, u_tab, v_tab,
           ln2_g, ln2_b):
    bsz, seq, d = x.shape
    depth = w_in.shape[0]
    t = bsz * seq
    assert d == N_HEADS * V_DIM and sub_keys.shape[1:] == (PK_HEADS, 2, N_KEYS, N_KEYS)
    tm = min(512, seq)
    tq = min(256, seq)
    te = 1024
    alpha = (2 * depth) ** 0.25
    row = lambda a: a.reshape(1, -1).astype(F32)
    freq_row = _rope_freq_row()
    pos2 = positions.reshape(t, 1)
    x2 = x.reshape(t, d)
    for l in range(depth):
        lam_init = 0.8 - 0.6 * math.exp(-0.3 * l)
        w_l = w_in[l].astype(BF16)
        w_hbc, w_qkv, w_g = w_l[:, :3 * d], w_l[:, 3 * d:6 * d], w_l[:, 6 * d:]
        q, k, v = _qkv_rope(x2, pos2, freq_row, w_qkv, tm)
        y_attn = _attention(q, k, v, row(lambda_q1[l]), row(lambda_k1[l]), row(lambda_q2[l]),
                            row(lambda_k2[l]), row(subln_g[l]), bsz, seq, tq, lam_init)
        x1, x1t = _mix(x2, y_attn, w_hbc, w_g, w_conv_out[l].astype(BF16),
                       w_attn_out[l].astype(BF16), w_o[l].astype(BF16), conv_w[l].astype(F32),
                       row(conv_b[l]), row(gate_b[l]), row(ln1_g[l]), row(ln1_b[l]),
                       tm, seq, alpha)
        keys = sub_keys[l].reshape(PK_HEADS * 2, N_KEYS, N_KEYS).astype(BF16)
        e1, thr, s2, e2 = _route(x1t, w_pq[l].T.astype(BF16), keys, min(256, tm))
        x2 = _peer(x1, x1t, u_tab[l].astype(BF16), v_tab[l].T.astype(BF16), e1, thr, s2, e2,
                   row(ln2_g[l]), row(ln2_b[l]), tm, te, alpha)
    return x2.reshape(bsz, seq, d)
```

```python
import functools
import math

import jax
import jax.numpy as jnp
from jax import lax
from jax.experimental import pallas as pl
from jax.experimental.pallas import tpu as pltpu

N_HEADS = 8
HEAD_DIM = 64
V_DIM = 2 * HEAD_DIM
ROT_DIM = HEAD_DIM // 4
ROPE_THETA = 500000.0
CONV_K = 3
N_KEYS = 128
PK_HEADS = 8
PK_TOPK = 16
LN_EPS = 1e-5
CONV_HALO = 16

VMEM_LIMIT_BYTES = 52 * 1024 * 1024

F32 = jnp.float32
BF16 = jnp.bfloat16


def _resident(shape):
    nd = len(shape)
    return pl.BlockSpec(shape, lambda *_: (0,) * nd, pipeline_mode=pl.Buffered(1))


def _layernorm(r, g, b):
    mu = jnp.mean(r, axis=-1, keepdims=True)
    c = r - mu
    var = jnp.mean(c * c, axis=-1, keepdims=True)
    return c * lax.rsqrt(var + LN_EPS) * g + b


def _qkv_rope_kernel(x_ref, pos_ref, freq_ref, w_ref, q_ref, k_ref, v_ref):
    d_model = x_ref.shape[1]
    proj = jnp.dot(x_ref[...].astype(BF16), w_ref[...], preferred_element_type=F32)
    ang = pos_ref[...].astype(F32) * freq_ref[...]
    cos = jnp.cos(ang)
    sin = jnp.sin(ang)
    d = lax.broadcasted_iota(jnp.int32, ang.shape, 1) & (HEAD_DIM - 1)
    half = ROT_DIM // 2
    lo = d < half
    hi = jnp.logical_and(d >= half, d < ROT_DIM)
    c_tab = jnp.where(d < ROT_DIM, cos, 1.0)
    s_lo = jnp.where(lo, -sin, 0.0)
    s_hi = jnp.where(hi, sin, 0.0)
    scale = HEAD_DIM ** -0.5 * math.log2(math.e)
    for h in range(N_HEADS):
        for off, ref, mul in ((0, q_ref, scale), (d_model, k_ref, 1.0)):
            t = proj[:, off + h * V_DIM: off + (h + 1) * V_DIM]
            up = jnp.where(lo, pltpu.roll(t, V_DIM - half, 1), 0.0)
            dn = jnp.where(hi, pltpu.roll(t, half, 1), 0.0)
            r = t * c_tab + up * s_lo + dn * s_hi
            ref[:, h * V_DIM:(h + 1) * V_DIM] = (r * mul).astype(BF16)
    v_ref[...] = proj[:, 2 * d_model:].astype(BF16)


def _qkv_rope(x2, pos2, freq_row, w_qkv, tm):
    t, d = x2.shape
    out = jax.ShapeDtypeStruct((t, d), BF16)
    tok = pl.BlockSpec((tm, d), lambda i: (i, 0))
    return pl.pallas_call(
        _qkv_rope_kernel,
        grid=(t // tm,),
        in_specs=[tok, pl.BlockSpec((tm, 1), lambda i: (i, 0)), _resident(freq_row.shape),
                  _resident(w_qkv.shape)],
        out_specs=[tok, tok, tok],
        out_shape=[out, out, out],
        name="qkv_rope",
        compiler_params=pltpu.CompilerParams(dimension_semantics=("parallel",),
                                             vmem_limit_bytes=VMEM_LIMIT_BYTES),
    )(x2, pos2, freq_row, w_qkv)


def _attn_tile(q_ref, k_ref, v_ref, g_ref, o_ref, lam, *, n, tq, lam_init):
    q = q_ref[...]
    lane_q = lax.broadcasted_iota(jnp.int32, q.shape, 1)
    zero = jnp.zeros_like(q)
    qq = jnp.concatenate([jnp.where(lane_q < HEAD_DIM, q, zero),
                          jnp.where(lane_q >= HEAD_DIM, q, zero)], axis=0)
    s = lax.dot_general(qq, k_ref[0:n, :], (((1,), (1,)), ((), ())),
                        preferred_element_type=F32)
    n_tiles = n // V_DIM
    diag0 = (n - tq) // V_DIM
    row = lax.broadcasted_iota(jnp.int32, (2 * tq, V_DIM), 0)
    qpos = jnp.where(row >= tq, row - tq, row)
    lane = lax.broadcasted_iota(jnp.int32, (2 * tq, V_DIM), 1)
    tiles = []
    for c in range(n_tiles):
        t = s[:, c * V_DIM:(c + 1) * V_DIM]
        if c >= diag0:
            t = jnp.where(lane + (c - diag0) * V_DIM <= qpos, t, -jnp.inf)
        tiles.append(t)
    m = jnp.max(functools.reduce(jnp.maximum, tiles), axis=-1, keepdims=True)
    p = [jnp.exp2(t - m) for t in tiles]
    r = 1.0 / jnp.sum(functools.reduce(lambda a, b: a + b, p), axis=-1, keepdims=True)
    r1 = r[:tq]
    r2 = lam * r[tq:]
    a = jnp.concatenate([(pt[:tq] * r1 - pt[tq:] * r2).astype(BF16) for pt in p], axis=1)
    o = jnp.dot(a, v_ref[0:n, :], preferred_element_type=F32)
    ms = jnp.mean(o * o, axis=-1, keepdims=True)
    y = o * lax.rsqrt(ms + LN_EPS) * g_ref[...] * (1.0 - lam_init)
    o_ref[...] = y.astype(o_ref.dtype)


def _attn_kernel(lq1_ref, lk1_ref, lq2_ref, lk2_ref, g_ref, q_ref, k_ref, v_ref, o_ref,
                 *, tq, nq, lam_init):
    i = pl.program_id(2)
    lam = (jnp.exp(jnp.sum(lq1_ref[...] * lk1_ref[...], axis=-1, keepdims=True))
           - jnp.exp(jnp.sum(lq2_ref[...] * lk2_ref[...], axis=-1, keepdims=True)) + lam_init)
    for nb in range(1, nq + 1):
        @pl.when(i == nb - 1)
        def _(nb=nb):
            _attn_tile(q_ref, k_ref, v_ref, g_ref, o_ref, lam, n=nb * tq, tq=tq, lam_init=lam_init)


def _attention(q, k, v, lq1, lk1, lq2, lk2, subln_g, bsz, seq, tq, lam_init):
    t, d = q.shape
    nq = seq // tq
    vec = lambda a: _resident(a.shape)
    return pl.pallas_call(
        functools.partial(_attn_kernel, tq=tq, nq=nq, lam_init=lam_init),
        grid=(bsz, N_HEADS, nq),
        in_specs=[vec(lq1), vec(lk1), vec(lq2), vec(lk2), vec(subln_g),
                  pl.BlockSpec((tq, V_DIM), lambda b, h, i: (b * nq + i, h)),
                  pl.BlockSpec((seq, V_DIM), lambda b, h, i: (b, h)),
                  pl.BlockSpec((seq, V_DIM), lambda b, h, i: (b, h))],
        out_specs=pl.BlockSpec((tq, V_DIM), lambda b, h, i: (b * nq + i, h)),
        out_shape=jax.ShapeDtypeStruct((t, d), BF16),
        name="diff_attn",
        compiler_params=pltpu.CompilerParams(
            dimension_semantics=("parallel", "parallel", "arbitrary"),
            vmem_limit_bytes=VMEM_LIMIT_BYTES),
    )(lq1, lk1, lq2, lk2, subln_g, q, k, v)


def _mix_kernel(xp_ref, x_ref, ya_ref, w_hbc_ref, w_g_ref, w_co_ref, w_ao_ref, w_o_ref,
                cw_ref, cb_ref, gb_ref, g1_ref, b1_ref, x1_ref, x1t_ref, z_sc,
                *, tm, tiles_per_seq, alpha):
    i = pl.program_id(0)
    d = x_ref.shape[1]
    x = x_ref[...]
    xcat = jnp.concatenate([xp_ref[...], x], axis=0).astype(BF16)
    hbc = jnp.dot(xcat, w_hbc_ref[...], preferred_element_type=F32)
    z = hbc[:, :d] * hbc[:, 2 * d:]
    row = lax.broadcasted_iota(jnp.int32, z.shape, 0)
    seq_start = (i % tiles_per_seq) == 0
    z_sc[...] = jnp.where(jnp.logical_and(row < CONV_HALO, seq_start), 0.0, z)
    cw = cw_ref[...]
    y = cb_ref[...] + cw[CONV_K - 1:CONV_K] * z_sc[pl.ds(CONV_HALO, tm), :]
    for j in range(CONV_K - 1):
        y = y + cw[j:j + 1] * z_sc[pl.ds(CONV_HALO - (CONV_K - 1) + j, tm), :]
    y_conv = hbc[CONV_HALO:, d:2 * d] * y
    c_out = jnp.dot(y_conv.astype(BF16), w_co_ref[...], preferred_element_type=F32)
    a_out = jnp.dot(ya_ref[...], w_ao_ref[...], preferred_element_type=F32)
    gates = jax.nn.sigmoid(
        jnp.dot(xcat[CONV_HALO:], w_g_ref[...], preferred_element_type=F32) + gb_ref[...])
    merged = gates[:, :d] * c_out + gates[:, d:] * a_out
    r = alpha * x + jnp.dot(merged.astype(BF16), w_o_ref[...], preferred_element_type=F32)
    x1 = _layernorm(r, g1_ref[...], b1_ref[...])
    x1_ref[...] = x1
    x1t_ref[...] = x1.T.astype(BF16)


def _mix(x2, y_attn, w_hbc, w_g, w_co, w_ao, w_o, conv_w, conv_b, gate_b, ln_g, ln_b,
         tm, seq, alpha):
    t, d = x2.shape
    halo_blocks = tm // CONV_HALO
    tok = pl.BlockSpec((tm, d), lambda i: (i, 0))
    prev = pl.BlockSpec((CONV_HALO, d), lambda i: (jnp.maximum(i * halo_blocks - 1, 0), 0))
    res = lambda a: _resident(a.shape)
    return pl.pallas_call(
        functools.partial(_mix_kernel, tm=tm, tiles_per_seq=seq // tm, alpha=alpha),
        grid=(t // tm,),
        in_specs=[prev, tok, tok, res(w_hbc), res(w_g), res(w_co), res(w_ao), res(w_o),
                  res(conv_w), res(conv_b), res(gate_b), res(ln_g), res(ln_b)],
        out_specs=[tok, pl.BlockSpec((d, tm), lambda i: (0, i))],
        out_shape=[jax.ShapeDtypeStruct((t, d), F32), jax.ShapeDtypeStruct((d, t), BF16)],
        scratch_shapes=[pltpu.VMEM((CONV_HALO + tm, d), F32)],
        name="conv_merge_ln1",
        compiler_params=pltpu.CompilerParams(dimension_semantics=("parallel",),
                                             vmem_limit_bytes=VMEM_LIMIT_BYTES),
    )(x2, x2, y_attn, w_hbc, w_g, w_co, w_ao, w_o, conv_w, conv_b, gate_b, ln_g, ln_b)


def _top_values(s, k):
    out = []
    for it in range(k):
        m = jnp.max(s, axis=0, keepdims=True)
        out.append(m)
        if it + 1 < k:
            s = jnp.where(s == m, -jnp.inf, s)
    return out


def _route_kernel(x1t_ref, wpqt_ref, keys_ref, e1_ref, cnt_ref, rank_ref, e2_ref):
    qt = jnp.dot(wpqt_ref[...], x1t_ref[...], preferred_element_type=F32).astype(BF16)
    inf = jnp.inf
    k = PK_TOPK
    hk = k // 2
    for h in range(PK_HEADS):
        s1 = jnp.dot(keys_ref[2 * h], qt[(2 * h) * N_KEYS:(2 * h + 1) * N_KEYS],
                     preferred_element_type=F32)
        s2 = jnp.dot(keys_ref[2 * h + 1], qt[(2 * h + 1) * N_KEYS:(2 * h + 2) * N_KEYS],
                     preferred_element_type=F32)
        m1 = _top_values(s1, k)
        m2 = _top_values(s2, k)
        m2a = jnp.concatenate(m2[:hk], axis=0)
        m2b = jnp.concatenate(m2[hk:], axis=0)
        m1b = jnp.concatenate(m1[hk:], axis=0)
        cands = [m1[0] + m2a, m1[0] + m2b] + [m1[a] + m2a for a in range(1, hk)] + [m1b + m2[0]]
        top = m1[0] + m2[0]
        work = list(cands)
        theta = top
        for it in range(k):
            mx = functools.reduce(jnp.maximum, work)
            theta = jnp.max(mx, axis=0, keepdims=True)
            if it + 1 < k:
                work = [jnp.where(w == theta, -inf, w) for w in work]
        sel = [c >= theta for c in cands]
        z = functools.reduce(
            lambda a, b: a + b,
            [jnp.sum(jnp.where(sl, jnp.exp(c - top), 0.0), axis=0, keepdims=True)
             for sl, c in zip(sel, cands)])
        ones = [jnp.where(sl, 1.0, 0.0) for sl in sel]
        n_sel = [jnp.sum(ones[0] + ones[1], axis=0, keepdims=True)]
        n_sel += [jnp.sum(ones[a + 1], axis=0, keepdims=True) for a in range(1, hk)]
        n_sel_b = ones[hk + 1]
        cnt = jnp.zeros(s1.shape, F32)
        for a in range(hk):
            cnt = jnp.where(s1 == m1[a], n_sel[a], cnt)
        for r in range(k - hk):
            cnt = jnp.where(s1 == m1[hk + r], n_sel_b[r:r + 1], cnt)
        rank = jnp.full(s2.shape, float(k), F32)
        for b in reversed(range(k)):
            rank = jnp.where(s2 >= m2[b], float(b), rank)
        e1_ref[h] = jnp.exp(s1 - m1[0]) / z
        cnt_ref[h] = cnt
        rank_ref[h] = rank.astype(rank_ref.dtype)
        e2_ref[h] = jnp.exp(s2 - m2[0]).astype(e2_ref.dtype)


def _route(x1t, wpqt, keys, tm):
    d, t = x1t.shape
    blk = pl.BlockSpec((PK_HEADS, N_KEYS, tm), lambda i: (0, 0, i))
    out = lambda dt: jax.ShapeDtypeStruct((PK_HEADS, N_KEYS, t), dt)
    return pl.pallas_call(
        _route_kernel,
        grid=(t // tm,),
        in_specs=[pl.BlockSpec((d, tm), lambda i: (0, i)), _resident(wpqt.shape),
                  _resident(keys.shape)],
        out_specs=[blk, blk, blk, blk],
        out_shape=[out(F32), out(F32), out(BF16), out(BF16)],
        name="peer_route",
        compiler_params=pltpu.CompilerParams(dimension_semantics=("parallel",),
                                             vmem_limit_bytes=VMEM_LIMIT_BYTES),
    )(x1t, wpqt, keys)


def _gate_chunk(tile, cols, e1_ref, cnt_ref, rank_ref, e2_ref, h_rd, w_sc, *, te):
    inv_sqrt2 = 1.0 / math.sqrt(2.0)
    n_i = te // N_KEYS
    width = cols.stop - cols.start
    zero = jnp.zeros((), BF16)
    for ii in range(n_i):
        rows = slice(ii * N_KEYS, (ii + 1) * N_KEYS)
        key = tile * n_i + ii
        g = jnp.zeros((N_KEYS, width), BF16)
        for h in range(PK_HEADS):
            cnt_row = cnt_ref[h, pl.ds(key, 1), cols].astype(BF16)
            e1_row = e1_ref[h, pl.ds(key, 1), cols].astype(BF16)
            g = g + e1_row * jnp.where(rank_ref[h, :, cols] < cnt_row, e2_ref[h, :, cols], zero)
        hh = h_rd[rows, cols]
        act = 0.5 * hh * (1.0 + lax.erf(hh * inv_sqrt2))
        w_sc[rows, cols] = act.astype(BF16) * g


def _peer_kernel(x1t_ref, u_ref, vt_ref, e1_ref, cnt_ref, rank_ref, e2_ref, x1_ref, g2_ref, b2_ref,
                 o_ref, h0_sc, h1_sc, w_sc, acc_sc, *, te, tm, n_e, lane_chunk, alpha):
    s = pl.program_id(1)

    @pl.when(s == 0)
    def _():
        acc_sc[...] = jnp.zeros(acc_sc.shape, F32)
        h1_sc[...] = jnp.zeros(h1_sc.shape, F32)

    tile = jnp.maximum(s - 1, 0)

    def step(h_wr, h_rd):
        for c0 in range(0, tm, lane_chunk):
            cols = slice(c0, c0 + lane_chunk)
            h_wr[:, cols] = jnp.dot(u_ref[...], x1t_ref[:, cols], preferred_element_type=F32)
            _gate_chunk(tile, cols, e1_ref, cnt_ref, rank_ref, e2_ref, h_rd, w_sc, te=te)
            acc_sc[:, cols] += jnp.dot(vt_ref[...], w_sc[:, cols], preferred_element_type=F32)

    @pl.when(s % 2 == 0)
    def _():
        step(h0_sc, h1_sc)

    @pl.when(s % 2 == 1)
    def _():
        step(h1_sc, h0_sc)

    @pl.when(s == n_e)
    def _():
        r = alpha * x1_ref[...] + acc_sc[...].T
        o_ref[...] = _layernorm(r, g2_ref[...], b2_ref[...])


def _peer(x1, x1t, u_bf, vt_bf, e1, cnt, rank, e2, ln_g, ln_b, tm, te, alpha):
    t, d = x1.shape
    n_e = u_bf.shape[0] // te
    all_blk = pl.BlockSpec((PK_HEADS, N_KEYS, tm), lambda i, s: (0, 0, i))
    return pl.pallas_call(
        functools.partial(_peer_kernel, te=te, tm=tm, n_e=n_e, lane_chunk=min(tm, 256),
                          alpha=alpha),
        grid=(t // tm, n_e + 1),
        in_specs=[pl.BlockSpec((d, tm), lambda i, s: (0, i)),
                  pl.BlockSpec((te, d), lambda i, s: (jnp.minimum(s, n_e - 1), 0)),
                  pl.BlockSpec((d, te), lambda i, s: (0, jnp.maximum(s - 1, 0))),
                  all_blk, all_blk, all_blk, all_blk,
                  pl.BlockSpec((tm, d), lambda i, s: (i, 0)),
                  _resident(ln_g.shape), _resident(ln_b.shape)],
        out_specs=pl.BlockSpec((tm, d), lambda i, s: (i, 0)),
        out_shape=jax.ShapeDtypeStruct((t, d), F32),
        scratch_shapes=[pltpu.VMEM((te, tm), F32), pltpu.VMEM((te, tm), F32),
                        pltpu.VMEM((te, tm), BF16), pltpu.VMEM((d, tm), F32)],
        name="peer_experts",
        compiler_params=pltpu.CompilerParams(dimension_semantics=("parallel", "arbitrary"),
                                             vmem_limit_bytes=VMEM_LIMIT_BYTES),
    )(x1t, u_bf, vt_bf, e1, cnt, rank, e2, x1, ln_g, ln_b)


def _rope_freq_row():
    inv_freq = ROPE_THETA ** (-jnp.arange(0, ROT_DIM, 2, dtype=F32) / ROT_DIM)
    d = jnp.arange(V_DIM) % HEAD_DIM
    return jnp.where(d < ROT_DIM, inv_freq[d % (ROT_DIM // 2)], 0.0).astype(F32)[None, :]


def kernel(x, positions, w_in, conv_w, conv_b, gate_b, lambda_q1, lambda_k1, lambda_q2, lambda_k2,
           subln_g, w_conv_out, w_attn_out, w_o, ln1_g, ln1_b, w_pq, sub_keys, u_tab, v_tab,
           ln2_g, ln2_b):
    bsz, seq, d = x.shape
    depth = w_in.shape[0]
    t = bsz * seq
    assert d == N_HEADS * V_DIM and sub_keys.shape[1:] == (PK_HEADS, 2, N_KEYS, N_KEYS)
    tm = min(512, seq)
    tq = min(256, seq)
    te = 512
    alpha = (2 * depth) ** 0.25
    row = lambda a: a.reshape(1, -1).astype(F32)
    freq_row = _rope_freq_row()
    pos2 = positions.reshape(t, 1)
    x2 = x.reshape(t, d)
    for l in range(depth):
        lam_init = 0.8 - 0.6 * math.exp(-0.3 * l)
        w_l = w_in[l].astype(BF16)
        w_hbc, w_qkv, w_g = w_l[:, :3 * d], w_l[:, 3 * d:6 * d], w_l[:, 6 * d:]
        q, k, v = _qkv_rope(x2, pos2, freq_row, w_qkv, tm)
        y_attn = _attention(q, k, v, row(lambda_q1[l]), row(lambda_k1[l]), row(lambda_q2[l]),
                            row(lambda_k2[l]), row(subln_g[l]), bsz, seq, tq, lam_init)
        x1, x1t = _mix(x2, y_attn, w_hbc, w_g, w_conv_out[l].astype(BF16),
                       w_attn_out[l].astype(BF16), w_o[l].astype(BF16), conv_w[l].astype(F32),
                       row(conv_b[l]), row(gate_b[l]), row(ln1_g[l]), row(ln1_b[l]),
                       tm, seq, alpha)
        keys = sub_keys[l].reshape(PK_HEADS * 2, N_KEYS, N_KEYS).astype(BF16)
        e1, cnt, rank, e2 = _route(x1t, w_pq[l].T.astype(BF16), keys, min(256, tm))
        x2 = _peer(x1, x1t, u_tab[l].astype(BF16), v_tab[l].T.astype(BF16), e1, cnt, rank, e2,
                   row(ln2_g[l]), row(ln2_b[l]), tm, te, alpha)
    return x2.reshape(bsz, seq, d)
```

```python
import functools
import math

import jax
import jax.numpy as jnp
from jax import lax
from jax.experimental import pallas as pl
from jax.experimental.pallas import tpu as pltpu

N_HEADS = 8
HEAD_DIM = 64
V_DIM = 2 * HEAD_DIM
ROT_DIM = HEAD_DIM // 4
ROPE_THETA = 500000.0
CONV_K = 3
N_KEYS = 128
PK_HEADS = 8
PK_TOPK = 16
LN_EPS = 1e-5
CONV_HALO = 16

VMEM_LIMIT_BYTES = 52 * 1024 * 1024

F32 = jnp.float32
BF16 = jnp.bfloat16


def _resident(shape):
    nd = len(shape)
    return pl.BlockSpec(shape, lambda *_: (0,) * nd, pipeline_mode=pl.Buffered(1))


def _layernorm(r, g, b):
    mu = jnp.mean(r, axis=-1, keepdims=True)
    c = r - mu
    var = jnp.mean(c * c, axis=-1, keepdims=True)
    return c * lax.rsqrt(var + LN_EPS) * g + b


def _qkv_rope_kernel(x_ref, pos_ref, freq_ref, w_ref, q_ref, k_ref, v_ref):
    d_model = x_ref.shape[1]
    proj = jnp.dot(x_ref[...].astype(BF16), w_ref[...], preferred_element_type=F32)
    ang = pos_ref[...].astype(F32) * freq_ref[...]
    cos = jnp.cos(ang)
    sin = jnp.sin(ang)
    d = lax.broadcasted_iota(jnp.int32, ang.shape, 1) & (HEAD_DIM - 1)
    half = ROT_DIM // 2
    lo = d < half
    hi = jnp.logical_and(d >= half, d < ROT_DIM)
    c_tab = jnp.where(d < ROT_DIM, cos, 1.0)
    s_lo = jnp.where(lo, -sin, 0.0)
    s_hi = jnp.where(hi, sin, 0.0)
    scale = HEAD_DIM ** -0.5 * math.log2(math.e)
    for h in range(N_HEADS):
        for off, ref, mul in ((0, q_ref, scale), (d_model, k_ref, 1.0)):
            t = proj[:, off + h * V_DIM: off + (h + 1) * V_DIM]
            up = jnp.where(lo, pltpu.roll(t, V_DIM - half, 1), 0.0)
            dn = jnp.where(hi, pltpu.roll(t, half, 1), 0.0)
            r = t * c_tab + up * s_lo + dn * s_hi
            ref[:, h * V_DIM:(h + 1) * V_DIM] = (r * mul).astype(BF16)
    v_ref[...] = proj[:, 2 * d_model:].astype(BF16)


def _qkv_rope(x2, pos2, freq_row, w_qkv, tm):
    t, d = x2.shape
    out = jax.ShapeDtypeStruct((t, d), BF16)
    tok = pl.BlockSpec((tm, d), lambda i: (i, 0))
    return pl.pallas_call(
        _qkv_rope_kernel,
        grid=(t // tm,),
        in_specs=[tok, pl.BlockSpec((tm, 1), lambda i: (i, 0)), _resident(freq_row.shape),
                  _resident(w_qkv.shape)],
        out_specs=[tok, tok, tok],
        out_shape=[out, out, out],
        name="qkv_rope",
        compiler_params=pltpu.CompilerParams(dimension_semantics=("parallel",),
                                             vmem_limit_bytes=VMEM_LIMIT_BYTES),
    )(x2, pos2, freq_row, w_qkv)


def _attn_tile(q_ref, k_ref, v_ref, g_ref, o_ref, lam, *, n, tq, lam_init):
    q = q_ref[...]
    lane_q = lax.broadcasted_iota(jnp.int32, q.shape, 1)
    zero = jnp.zeros_like(q)
    qq = jnp.concatenate([jnp.where(lane_q < HEAD_DIM, q, zero),
                          jnp.where(lane_q >= HEAD_DIM, q, zero)], axis=0)
    s = lax.dot_general(qq, k_ref[0:n, :], (((1,), (1,)), ((), ())),
                        preferred_element_type=F32)
    n_tiles = n // V_DIM
    diag0 = (n - tq) // V_DIM
    row = lax.broadcasted_iota(jnp.int32, (2 * tq, V_DIM), 0)
    qpos = jnp.where(row >= tq, row - tq, row)
    lane = lax.broadcasted_iota(jnp.int32, (2 * tq, V_DIM), 1)
    tiles = []
    for c in range(n_tiles):
        t = s[:, c * V_DIM:(c + 1) * V_DIM]
        if c >= diag0:
            t = jnp.where(lane + (c - diag0) * V_DIM <= qpos, t, -jnp.inf)
        tiles.append(t)
    m = jnp.max(functools.reduce(jnp.maximum, tiles), axis=-1, keepdims=True)
    p = [jnp.exp2(t - m) for t in tiles]
    r = 1.0 / jnp.sum(functools.reduce(lambda a, b: a + b, p), axis=-1, keepdims=True)
    r1 = r[:tq]
    r2 = lam * r[tq:]
    a = jnp.concatenate([(pt[:tq] * r1 - pt[tq:] * r2).astype(BF16) for pt in p], axis=1)
    o = jnp.dot(a, v_ref[0:n, :], preferred_element_type=F32)
    ms = jnp.mean(o * o, axis=-1, keepdims=True)
    y = o * lax.rsqrt(ms + LN_EPS) * g_ref[...] * (1.0 - lam_init)
    o_ref[...] = y.astype(o_ref.dtype)


def _attn_kernel(lq1_ref, lk1_ref, lq2_ref, lk2_ref, g_ref, q_ref, k_ref, v_ref, o_ref,
                 *, tq, nq, lam_init):
    i = pl.program_id(2)
    lam = (jnp.exp(jnp.sum(lq1_ref[...] * lk1_ref[...], axis=-1, keepdims=True))
           - jnp.exp(jnp.sum(lq2_ref[...] * lk2_ref[...], axis=-1, keepdims=True)) + lam_init)
    for nb in range(1, nq + 1):
        @pl.when(i == nb - 1)
        def _(nb=nb):
            _attn_tile(q_ref, k_ref, v_ref, g_ref, o_ref, lam, n=nb * tq, tq=tq, lam_init=lam_init)


def _attention(q, k, v, lq1, lk1, lq2, lk2, subln_g, bsz, seq, tq, lam_init):
    t, d = q.shape
    nq = seq // tq
    vec = lambda a: _resident(a.shape)
    return pl.pallas_call(
        functools.partial(_attn_kernel, tq=tq, nq=nq, lam_init=lam_init),
        grid=(bsz, N_HEADS, nq),
        in_specs=[vec(lq1), vec(lk1), vec(lq2), vec(lk2), vec(subln_g),
                  pl.BlockSpec((tq, V_DIM), lambda b, h, i: (b * nq + i, h)),
                  pl.BlockSpec((seq, V_DIM), lambda b, h, i: (b, h)),
                  pl.BlockSpec((seq, V_DIM), lambda b, h, i: (b, h))],
        out_specs=pl.BlockSpec((tq, V_DIM), lambda b, h, i: (b * nq + i, h)),
        out_shape=jax.ShapeDtypeStruct((t, d), BF16),
        name="diff_attn",
        compiler_params=pltpu.CompilerParams(
            dimension_semantics=("parallel", "parallel", "arbitrary"),
            vmem_limit_bytes=VMEM_LIMIT_BYTES),
    )(lq1, lk1, lq2, lk2, subln_g, q, k, v)


def _mix_kernel(xp_ref, x_ref, ya_ref, w_hbc_ref, w_g_ref, w_co_ref, w_ao_ref, w_o_ref,
                cw_ref, cb_ref, gb_ref, g1_ref, b1_ref, x1_ref, x1t_ref, z_sc,
                *, tm, tiles_per_seq, alpha):
    i = pl.program_id(0)
    d = x_ref.shape[1]
    x = x_ref[...]
    xcat = jnp.concatenate([xp_ref[...], x], axis=0).astype(BF16)
    hbc = jnp.dot(xcat, w_hbc_ref[...], preferred_element_type=F32)
    z = hbc[:, :d] * hbc[:, 2 * d:]
    row = lax.broadcasted_iota(jnp.int32, z.shape, 0)
    seq_start = (i % tiles_per_seq) == 0
    z_sc[...] = jnp.where(jnp.logical_and(row < CONV_HALO, seq_start), 0.0, z)
    cw = cw_ref[...]
    y = cb_ref[...] + cw[CONV_K - 1:CONV_K] * z_sc[pl.ds(CONV_HALO, tm), :]
    for j in range(CONV_K - 1):
        y = y + cw[j:j + 1] * z_sc[pl.ds(CONV_HALO - (CONV_K - 1) + j, tm), :]
    y_conv = hbc[CONV_HALO:, d:2 * d] * y
    c_out = jnp.dot(y_conv.astype(BF16), w_co_ref[...], preferred_element_type=F32)
    a_out = jnp.dot(ya_ref[...], w_ao_ref[...], preferred_element_type=F32)
    gates = jax.nn.sigmoid(
        jnp.dot(xcat[CONV_HALO:], w_g_ref[...], preferred_element_type=F32) + gb_ref[...])
    merged = gates[:, :d] * c_out + gates[:, d:] * a_out
    r = alpha * x + jnp.dot(merged.astype(BF16), w_o_ref[...], preferred_element_type=F32)
    x1 = _layernorm(r, g1_ref[...], b1_ref[...])
    x1_ref[...] = x1
    x1t_ref[...] = x1.T.astype(BF16)


def _mix(x2, y_attn, w_hbc, w_g, w_co, w_ao, w_o, conv_w, conv_b, gate_b, ln_g, ln_b,
         tm, seq, alpha):
    t, d = x2.shape
    halo_blocks = tm // CONV_HALO
    tok = pl.BlockSpec((tm, d), lambda i: (i, 0))
    prev = pl.BlockSpec((CONV_HALO, d), lambda i: (jnp.maximum(i * halo_blocks - 1, 0), 0))
    res = lambda a: _resident(a.shape)
    return pl.pallas_call(
        functools.partial(_mix_kernel, tm=tm, tiles_per_seq=seq // tm, alpha=alpha),
        grid=(t // tm,),
        in_specs=[prev, tok, tok, res(w_hbc), res(w_g), res(w_co), res(w_ao), res(w_o),
                  res(conv_w), res(conv_b), res(gate_b), res(ln_g), res(ln_b)],
        out_specs=[tok, pl.BlockSpec((d, tm), lambda i: (0, i))],
        out_shape=[jax.ShapeDtypeStruct((t, d), F32), jax.ShapeDtypeStruct((d, t), BF16)],
        scratch_shapes=[pltpu.VMEM((CONV_HALO + tm, d), F32)],
        name="conv_merge_ln1",
        compiler_params=pltpu.CompilerParams(dimension_semantics=("parallel",),
                                             vmem_limit_bytes=VMEM_LIMIT_BYTES),
    )(x2, x2, y_attn, w_hbc, w_g, w_co, w_ao, w_o, conv_w, conv_b, gate_b, ln_g, ln_b)


def _sort_network(n):
    pairs = []

    def merge(lo, hi, r):
        step = r * 2
        if step < hi - lo:
            merge(lo, hi, step)
            merge(lo + r, hi, step)
            pairs.extend((i, i + r) for i in range(lo + r, hi - r, step))
        else:
            pairs.append((lo, lo + r))

    def sort(lo, hi):
        if hi - lo >= 1:
            mid = lo + (hi - lo) // 2
            sort(lo, mid)
            sort(mid + 1, hi)
            merge(lo, hi, 1)

    sort(0, n - 1)
    return tuple(pairs)


SUBLANES = 8
_SORT_PAIRS = _sort_network(PK_TOPK)
_BITONIC_PAIRS = tuple((i, i + d) for d in (8, 4, 2, 1) for i in range(PK_TOPK) if not i & d)


def _compare_exchange(a, pairs):
    for i, j in pairs:
        a[i], a[j] = jnp.maximum(a[i], a[j]), jnp.minimum(a[i], a[j])


def _top_values(s):
    k = PK_TOPK
    assert s.shape[0] == k * SUBLANES
    a = [s[SUBLANES * v:SUBLANES * (v + 1)] for v in range(k)]
    _compare_exchange(a, _SORT_PAIRS)
    shift = SUBLANES // 2
    while shift:
        b = [pltpu.roll(x, shift, 0) for x in a]
        a = [jnp.maximum(a[i], b[k - 1 - i]) for i in range(k)]
        _compare_exchange(a, _BITONIC_PAIRS)
        shift //= 2
    return a


def _route_kernel(x1t_ref, wpqt_ref, keys_ref, e1_ref, cnt_ref, rank_ref, e2_ref):
    qt = jnp.dot(wpqt_ref[...], x1t_ref[...], preferred_element_type=F32).astype(BF16)
    inf = jnp.inf
    k = PK_TOPK
    hk = k // 2
    for h in range(PK_HEADS):
        s1 = jnp.dot(keys_ref[2 * h], qt[(2 * h) * N_KEYS:(2 * h + 1) * N_KEYS],
                     preferred_element_type=F32)
        s2 = jnp.dot(keys_ref[2 * h + 1], qt[(2 * h + 1) * N_KEYS:(2 * h + 2) * N_KEYS],
                     preferred_element_type=F32)
        m1_rep = _top_values(s1)
        m2_rep = _top_values(s2)
        m1 = [m[0:1] for m in m1_rep]
        m2 = [m[0:1] for m in m2_rep]
        sub = lax.broadcasted_iota(jnp.int32, m1_rep[0].shape, 0)

        def stack(reps):
            out = reps[0]
            for r in range(1, hk):
                out = jnp.where(sub == r, reps[r], out)
            return out

        m2a = stack(m2_rep[:hk])
        m2b = stack(m2_rep[hk:])
        m1b = stack(m1_rep[hk:])
        cands = ([m1_rep[0] + m2a, m1_rep[0] + m2b] + [m1_rep[a] + m2a for a in range(1, hk)]
                 + [m1b + m2_rep[0]])
        top = m1[0] + m2[0]
        work = list(cands)
        theta = top
        for it in range(k):
            mx = functools.reduce(jnp.maximum, work)
            theta = jnp.max(mx, axis=0, keepdims=True)
            if it + 1 < k:
                work = [jnp.where(w == theta, -inf, w) for w in work]
        sel = [c >= theta for c in cands]
        z = functools.reduce(
            lambda a, b: a + b,
            [jnp.sum(jnp.where(sl, jnp.exp(c - top), 0.0), axis=0, keepdims=True)
             for sl, c in zip(sel, cands)])
        ones = [jnp.where(sl, 1.0, 0.0) for sl in sel]
        n_sel = [jnp.sum(ones[0] + ones[1], axis=0, keepdims=True)]
        n_sel += [jnp.sum(ones[a + 1], axis=0, keepdims=True) for a in range(1, hk)]
        n_sel_b = ones[hk + 1]
        cnt = jnp.zeros(s1.shape, F32)
        for a in range(hk):
            cnt = jnp.where(s1 == m1[a], n_sel[a], cnt)
        for r in range(k - hk):
            cnt = jnp.where(s1 == m1[hk + r], n_sel_b[r:r + 1], cnt)
        rank = jnp.full(s2.shape, float(k), F32)
        for b in reversed(range(k)):
            rank = jnp.where(s2 >= m2[b], float(b), rank)
        e1_ref[h] = jnp.exp(s1 - m1[0]) / z
        cnt_ref[h] = cnt
        rank_ref[h] = rank.astype(rank_ref.dtype)
        e2_ref[h] = jnp.exp(s2 - m2[0]).astype(e2_ref.dtype)


def _route(x1t, wpqt, keys, tm):
    d, t = x1t.shape
    blk = pl.BlockSpec((PK_HEADS, N_KEYS, tm), lambda i: (0, 0, i))
    out = lambda dt: jax.ShapeDtypeStruct((PK_HEADS, N_KEYS, t), dt)
    return pl.pallas_call(
        _route_kernel,
        grid=(t // tm,),
        in_specs=[pl.BlockSpec((d, tm), lambda i: (0, i)), _resident(wpqt.shape),
                  _resident(keys.shape)],
        out_specs=[blk, blk, blk, blk],
        out_shape=[out(F32), out(F32), out(BF16), out(BF16)],
        name="peer_route",
        compiler_params=pltpu.CompilerParams(dimension_semantics=("parallel",),
                                             vmem_limit_bytes=VMEM_LIMIT_BYTES),
    )(x1t, wpqt, keys)


def _gate_chunk(tile, cols, e1_ref, cnt_ref, rank_ref, e2_ref, h_rd, w_sc, *, te):
    inv_sqrt2 = 1.0 / math.sqrt(2.0)
    n_i = te // N_KEYS
    width = cols.stop - cols.start
    zero = jnp.zeros((), BF16)
    for ii in range(n_i):
        rows = slice(ii * N_KEYS, (ii + 1) * N_KEYS)
        key = tile * n_i + ii
        g = jnp.zeros((N_KEYS, width), BF16)
        for h in range(PK_HEADS):
            cnt_row = cnt_ref[h, pl.ds(key, 1), cols].astype(BF16)
            e1_row = e1_ref[h, pl.ds(key, 1), cols].astype(BF16)
            g = g + e1_row * jnp.where(rank_ref[h, :, cols] < cnt_row, e2_ref[h, :, cols], zero)
        hh = h_rd[rows, cols]
        act = 0.5 * hh * (1.0 + lax.erf(hh * inv_sqrt2))
        w_sc[rows, cols] = act.astype(BF16) * g


def _peer_kernel(x1t_ref, u_ref, vt_ref, e1_ref, cnt_ref, rank_ref, e2_ref, x1_ref, g2_ref, b2_ref,
                 o_ref, h0_sc, h1_sc, w_sc, acc_sc, *, te, tm, n_e, lane_chunk, alpha):
    s = pl.program_id(1)

    @pl.when(s == 0)
    def _():
        acc_sc[...] = jnp.zeros(acc_sc.shape, F32)
        h1_sc[...] = jnp.zeros(h1_sc.shape, F32)

    tile = jnp.maximum(s - 1, 0)

    def step(h_wr, h_rd):
        for c0 in range(0, tm, lane_chunk):
            cols = slice(c0, c0 + lane_chunk)
            h_wr[:, cols] = jnp.dot(u_ref[...], x1t_ref[:, cols], preferred_element_type=F32)
            _gate_chunk(tile, cols, e1_ref, cnt_ref, rank_ref, e2_ref, h_rd, w_sc, te=te)
            acc_sc[:, cols] += jnp.dot(vt_ref[...], w_sc[:, cols], preferred_element_type=F32)

    @pl.when(s % 2 == 0)
    def _():
        step(h0_sc, h1_sc)

    @pl.when(s % 2 == 1)
    def _():
        step(h1_sc, h0_sc)

    @pl.when(s == n_e)
    def _():
        r = alpha * x1_ref[...] + acc_sc[...].T
        o_ref[...] = _layernorm(r, g2_ref[...], b2_ref[...])


def _peer(x1, x1t, u_bf, vt_bf, e1, cnt, rank, e2, ln_g, ln_b, tm, te, alpha):
    t, d = x1.shape
    n_e = u_bf.shape[0] // te
    all_blk = pl.BlockSpec((PK_HEADS, N_KEYS, tm), lambda i, s: (0, 0, i))
    return pl.pallas_call(
        functools.partial(_peer_kernel, te=te, tm=tm, n_e=n_e, lane_chunk=min(tm, 256),
                          alpha=alpha),
        grid=(t // tm, n_e + 1),
        in_specs=[pl.BlockSpec((d, tm), lambda i, s: (0, i)),
                  pl.BlockSpec((te, d), lambda i, s: (jnp.minimum(s, n_e - 1), 0)),
                  pl.BlockSpec((d, te), lambda i, s: (0, jnp.maximum(s - 1, 0))),
                  all_blk, all_blk, all_blk, all_blk,
                  pl.BlockSpec((tm, d), lambda i, s: (i, 0)),
                  _resident(ln_g.shape), _resident(ln_b.shape)],
        out_specs=pl.BlockSpec((tm, d), lambda i, s: (i, 0)),
        out_shape=jax.ShapeDtypeStruct((t, d), F32),
        scratch_shapes=[pltpu.VMEM((te, tm), F32), pltpu.VMEM((te, tm), F32),
                        pltpu.VMEM((te, tm), BF16), pltpu.VMEM((d, tm), F32)],
        name="peer_experts",
        compiler_params=pltpu.CompilerParams(dimension_semantics=("parallel", "arbitrary"),
                                             vmem_limit_bytes=VMEM_LIMIT_BYTES),
    )(x1t, u_bf, vt_bf, e1, cnt, rank, e2, x1, ln_g, ln_b)


def _rope_freq_row():
    inv_freq = ROPE_THETA ** (-jnp.arange(0, ROT_DIM, 2, dtype=F32) / ROT_DIM)
    d = jnp.arange(V_DIM) % HEAD_DIM
    return jnp.where(d < ROT_DIM, inv_freq[d % (ROT_DIM // 2)], 0.0).astype(F32)[None, :]


def kernel(x, positions, w_in, conv_w, conv_b, gate_b, lambda_q1, lambda_k1, lambda_q2, lambda_k2,
           subln_g, w_conv_out, w_attn_out, w_o, ln1_g, ln1_b, w_pq, sub_keys, u_tab, v_tab,
           ln2_g, ln2_b):
    bsz, seq, d = x.shape
    depth = w_in.shape[0]
    t = bsz * seq
    assert d == N_HEADS * V_DIM and sub_keys.shape[1:] == (PK_HEADS, 2, N_KEYS, N_KEYS)
    tm = min(512, seq)
    tq = min(256, seq)
    te = 1024
    alpha = (2 * depth) ** 0.25
    row = lambda a: a.reshape(1, -1).astype(F32)
    freq_row = _rope_freq_row()
    pos2 = positions.reshape(t, 1)
    x2 = x.reshape(t, d)
    for l in range(depth):
        lam_init = 0.8 - 0.6 * math.exp(-0.3 * l)
        w_l = w_in[l].astype(BF16)
        w_hbc, w_qkv, w_g = w_l[:, :3 * d], w_l[:, 3 * d:6 * d], w_l[:, 6 * d:]
        q, k, v = _qkv_rope(x2, pos2, freq_row, w_qkv, tm)
        y_attn = _attention(q, k, v, row(lambda_q1[l]), row(lambda_k1[l]), row(lambda_q2[l]),
                            row(lambda_k2[l]), row(subln_g[l]), bsz, seq, tq, lam_init)
        x1, x1t = _mix(x2, y_attn, w_hbc, w_g, w_conv_out[l].astype(BF16),
                       w_attn_out[l].astype(BF16), w_o[l].astype(BF16), conv_w[l].astype(F32),
                       row(conv_b[l]), row(gate_b[l]), row(ln1_g[l]), row(ln1_b[l]),
                       tm, seq, alpha)
        keys = sub_keys[l].reshape(PK_HEADS * 2, N_KEYS, N_KEYS).astype(BF16)
        e1, cnt, rank, e2 = _route(x1t, w_pq[l].T.astype(BF16), keys, min(256, tm))
        x2 = _peer(x1, x1t, u_tab[l].astype(BF16), v_tab[l].T.astype(BF16), e1, cnt, rank, e2,
                   row(ln2_g[l]), row(ln2_b[l]), tm, te, alpha)
    return x2.reshape(bsz, seq, d)
```

```python
import functools
import math

import jax
import jax.numpy as jnp
from jax import lax
from jax.experimental import pallas as pl
from jax.experimental.pallas import tpu as pltpu

N_HEADS = 8
HEAD_DIM = 64
V_DIM = 2 * HEAD_DIM
ROT_DIM = HEAD_DIM // 4
ROPE_THETA = 500000.0
CONV_K = 3
N_KEYS = 128
PK_HEADS = 8
PK_TOPK = 16
LN_EPS = 1e-5
CONV_HALO = 16

VMEM_LIMIT_BYTES = 52 * 1024 * 1024

F32 = jnp.float32
BF16 = jnp.bfloat16


def _resident(shape):
    nd = len(shape)
    return pl.BlockSpec(shape, lambda *_: (0,) * nd, pipeline_mode=pl.Buffered(1))


def _layernorm(r, g, b):
    mu = jnp.mean(r, axis=-1, keepdims=True)
    c = r - mu
    var = jnp.mean(c * c, axis=-1, keepdims=True)
    return c * lax.rsqrt(var + LN_EPS) * g + b


def _qkv_rope_kernel(x_ref, pos_ref, freq_ref, w_ref, q_ref, k_ref, v_ref):
    d_model = x_ref.shape[1]
    proj = jnp.dot(x_ref[...].astype(BF16), w_ref[...], preferred_element_type=F32)
    ang = pos_ref[...].astype(F32) * freq_ref[...]
    cos = jnp.cos(ang)
    sin = jnp.sin(ang)
    d = lax.broadcasted_iota(jnp.int32, ang.shape, 1) & (HEAD_DIM - 1)
    half = ROT_DIM // 2
    lo = d < half
    hi = jnp.logical_and(d >= half, d < ROT_DIM)
    c_tab = jnp.where(d < ROT_DIM, cos, 1.0)
    s_lo = jnp.where(lo, -sin, 0.0)
    s_hi = jnp.where(hi, sin, 0.0)
    scale = HEAD_DIM ** -0.5 * math.log2(math.e)
    for h in range(N_HEADS):
        for off, ref, mul in ((0, q_ref, scale), (d_model, k_ref, 1.0)):
            t = proj[:, off + h * V_DIM: off + (h + 1) * V_DIM]
            up = jnp.where(lo, pltpu.roll(t, V_DIM - half, 1), 0.0)
            dn = jnp.where(hi, pltpu.roll(t, half, 1), 0.0)
            r = t * c_tab + up * s_lo + dn * s_hi
            ref[:, h * V_DIM:(h + 1) * V_DIM] = (r * mul).astype(BF16)
    v_ref[...] = proj[:, 2 * d_model:].astype(BF16)


def _qkv_rope(x2, pos2, freq_row, w_qkv, tm):
    t, d = x2.shape
    out = jax.ShapeDtypeStruct((t, d), BF16)
    tok = pl.BlockSpec((tm, d), lambda i: (i, 0))
    return pl.pallas_call(
        _qkv_rope_kernel,
        grid=(t // tm,),
        in_specs=[tok, pl.BlockSpec((tm, 1), lambda i: (i, 0)), _resident(freq_row.shape),
                  _resident(w_qkv.shape)],
        out_specs=[tok, tok, tok],
        out_shape=[out, out, out],
        name="qkv_rope",
        compiler_params=pltpu.CompilerParams(dimension_semantics=("parallel",),
                                             vmem_limit_bytes=VMEM_LIMIT_BYTES),
    )(x2, pos2, freq_row, w_qkv)


def _attn_tile(q_ref, k_ref, v_ref, g_ref, o_ref, lam, *, n, tq, heads, lam_init):
    n_tiles = n // V_DIM
    diag0 = (n - tq) // V_DIM
    row = lax.broadcasted_iota(jnp.int32, (2 * tq, V_DIM), 0)
    qpos = jnp.where(row >= tq, row - tq, row)
    lane = lax.broadcasted_iota(jnp.int32, (2 * tq, V_DIM), 1)
    lane_q = lax.broadcasted_iota(jnp.int32, (tq, V_DIM), 1)
    cols = [slice(h * V_DIM, (h + 1) * V_DIM) for h in range(heads)]
    scores = []
    for c in cols:
        q = q_ref[:, c]
        zero = jnp.zeros_like(q)
        qq = jnp.concatenate([jnp.where(lane_q < HEAD_DIM, q, zero),
                              jnp.where(lane_q >= HEAD_DIM, q, zero)], axis=0)
        scores.append(lax.dot_general(qq, k_ref[0:n, c], (((1,), (1,)), ((), ())),
                                      preferred_element_type=F32))
    tiles = []
    for s in scores:
        ts = []
        for j in range(n_tiles):
            t = s[:, j * V_DIM:(j + 1) * V_DIM]
            if j >= diag0:
                t = jnp.where(lane + (j - diag0) * V_DIM <= qpos, t, -jnp.inf)
            ts.append(t)
        tiles.append(ts)
    mx = [jnp.max(functools.reduce(jnp.maximum, ts), axis=-1, keepdims=True) for ts in tiles]
    p = [[jnp.exp2(t - m) for t in ts] for ts, m in zip(tiles, mx)]
    r = [1.0 / jnp.sum(functools.reduce(lambda a, b: a + b, ps), axis=-1, keepdims=True)
         for ps in p]
    outs = []
    for ps, rh, c in zip(p, r, cols):
        r1 = rh[:tq]
        r2 = lam * rh[tq:]
        a = jnp.concatenate([(pt[:tq] * r1 - pt[tq:] * r2).astype(BF16) for pt in ps], axis=1)
        outs.append(jnp.dot(a, v_ref[0:n, c], preferred_element_type=F32))
    for o, c in zip(outs, cols):
        ms = jnp.mean(o * o, axis=-1, keepdims=True)
        y = o * lax.rsqrt(ms + LN_EPS) * g_ref[...] * (1.0 - lam_init)
        o_ref[:, c] = y.astype(o_ref.dtype)


def _attn_kernel(lq1_ref, lk1_ref, lq2_ref, lk2_ref, g_ref, q_ref, k_ref, v_ref, o_ref,
                 *, tq, nq, heads, lam_init):
    i = pl.program_id(2)
    lam = (jnp.exp(jnp.sum(lq1_ref[...] * lk1_ref[...], axis=-1, keepdims=True))
           - jnp.exp(jnp.sum(lq2_ref[...] * lk2_ref[...], axis=-1, keepdims=True)) + lam_init)
    for nb in range(1, nq + 1):
        @pl.when(i == nb - 1)
        def _(nb=nb):
            _attn_tile(q_ref, k_ref, v_ref, g_ref, o_ref, lam, n=nb * tq, tq=tq, heads=heads,
                       lam_init=lam_init)


def _attention(q, k, v, lq1, lk1, lq2, lk2, subln_g, bsz, seq, tq, heads, lam_init):
    t, d = q.shape
    nq = seq // tq
    width = heads * V_DIM
    vec = lambda a: _resident(a.shape)
    return pl.pallas_call(
        functools.partial(_attn_kernel, tq=tq, nq=nq, heads=heads, lam_init=lam_init),
        grid=(bsz, N_HEADS // heads, nq),
        in_specs=[vec(lq1), vec(lk1), vec(lq2), vec(lk2), vec(subln_g),
                  pl.BlockSpec((tq, width), lambda b, h, i: (b * nq + i, h)),
                  pl.BlockSpec((seq, width), lambda b, h, i: (b, h)),
                  pl.BlockSpec((seq, width), lambda b, h, i: (b, h))],
        out_specs=pl.BlockSpec((tq, width), lambda b, h, i: (b * nq + i, h)),
        out_shape=jax.ShapeDtypeStruct((t, d), BF16),
        name="diff_attn",
        compiler_params=pltpu.CompilerParams(
            dimension_semantics=("parallel", "parallel", "arbitrary"),
            vmem_limit_bytes=VMEM_LIMIT_BYTES),
    )(lq1, lk1, lq2, lk2, subln_g, q, k, v)


def _mix_kernel(xp_ref, x_ref, ya_ref, w_hbc_ref, w_g_ref, w_co_ref, w_ao_ref, w_o_ref,
                cw_ref, cb_ref, gb_ref, g1_ref, b1_ref, x1_ref, x1t_ref, z_sc,
                *, tm, tiles_per_seq, alpha):
    i = pl.program_id(0)
    d = x_ref.shape[1]
    x = x_ref[...]
    xcat = jnp.concatenate([xp_ref[...], x], axis=0).astype(BF16)
    hbc = jnp.dot(xcat, w_hbc_ref[...], preferred_element_type=F32)
    z = hbc[:, :d] * hbc[:, 2 * d:]
    row = lax.broadcasted_iota(jnp.int32, z.shape, 0)
    seq_start = (i % tiles_per_seq) == 0
    z_sc[...] = jnp.where(jnp.logical_and(row < CONV_HALO, seq_start), 0.0, z)
    cw = cw_ref[...]
    y = cb_ref[...] + cw[CONV_K - 1:CONV_K] * z_sc[pl.ds(CONV_HALO, tm), :]
    for j in range(CONV_K - 1):
        y = y + cw[j:j + 1] * z_sc[pl.ds(CONV_HALO - (CONV_K - 1) + j, tm), :]
    y_conv = hbc[CONV_HALO:, d:2 * d] * y
    c_out = jnp.dot(y_conv.astype(BF16), w_co_ref[...], preferred_element_type=F32)
    a_out = jnp.dot(ya_ref[...], w_ao_ref[...], preferred_element_type=F32)
    gates = jax.nn.sigmoid(
        jnp.dot(xcat[CONV_HALO:], w_g_ref[...], preferred_element_type=F32) + gb_ref[...])
    merged = gates[:, :d] * c_out + gates[:, d:] * a_out
    r = alpha * x + jnp.dot(merged.astype(BF16), w_o_ref[...], preferred_element_type=F32)
    x1 = _layernorm(r, g1_ref[...], b1_ref[...])
    x1_ref[...] = x1
    x1t_ref[...] = x1.T.astype(BF16)


def _mix(x2, y_attn, w_hbc, w_g, w_co, w_ao, w_o, conv_w, conv_b, gate_b, ln_g, ln_b,
         tm, seq, alpha):
    t, d = x2.shape
    halo_blocks = tm // CONV_HALO
    tok = pl.BlockSpec((tm, d), lambda i: (i, 0))
    prev = pl.BlockSpec((CONV_HALO, d), lambda i: (jnp.maximum(i * halo_blocks - 1, 0), 0))
    res = lambda a: _resident(a.shape)
    return pl.pallas_call(
        functools.partial(_mix_kernel, tm=tm, tiles_per_seq=seq // tm, alpha=alpha),
        grid=(t // tm,),
        in_specs=[prev, tok, tok, res(w_hbc), res(w_g), res(w_co), res(w_ao), res(w_o),
                  res(conv_w), res(conv_b), res(gate_b), res(ln_g), res(ln_b)],
        out_specs=[tok, pl.BlockSpec((d, tm), lambda i: (0, i))],
        out_shape=[jax.ShapeDtypeStruct((t, d), F32), jax.ShapeDtypeStruct((d, t), BF16)],
        scratch_shapes=[pltpu.VMEM((CONV_HALO + tm, d), F32)],
        name="conv_merge_ln1",
        compiler_params=pltpu.CompilerParams(dimension_semantics=("parallel",),
                                             vmem_limit_bytes=VMEM_LIMIT_BYTES),
    )(x2, x2, y_attn, w_hbc, w_g, w_co, w_ao, w_o, conv_w, conv_b, gate_b, ln_g, ln_b)


def _sort_network(n):
    pairs = []

    def merge(lo, hi, r):
        step = r * 2
        if step < hi - lo:
            merge(lo, hi, step)
            merge(lo + r, hi, step)
            pairs.extend((i, i + r) for i in range(lo + r, hi - r, step))
        else:
            pairs.append((lo, lo + r))

    def sort(lo, hi):
        if hi - lo >= 1:
            mid = lo + (hi - lo) // 2
            sort(lo, mid)
            sort(mid + 1, hi)
            merge(lo, hi, 1)

    sort(0, n - 1)
    return tuple(pairs)


SUBLANES = 8
_SORT_PAIRS = _sort_network(PK_TOPK)
_BITONIC_PAIRS = tuple((i, i + d) for d in (8, 4, 2, 1) for i in range(PK_TOPK) if not i & d)


def _compare_exchange(a, pairs):
    for i, j in pairs:
        a[i], a[j] = jnp.maximum(a[i], a[j]), jnp.minimum(a[i], a[j])


def _top_values(s):
    k = PK_TOPK
    assert s.shape[0] == k * SUBLANES
    a = [s[SUBLANES * v:SUBLANES * (v + 1)] for v in range(k)]
    _compare_exchange(a, _SORT_PAIRS)
    shift = SUBLANES // 2
    while shift:
        b = [pltpu.roll(x, shift, 0) for x in a]
        a = [jnp.maximum(a[i], b[k - 1 - i]) for i in range(k)]
        _compare_exchange(a, _BITONIC_PAIRS)
        shift //= 2
    return a


def _route_kernel(x1t_ref, wpqt_ref, keys_ref, e1_ref, cnt_ref, rank_ref, e2_ref):
    qt = jnp.dot(wpqt_ref[...], x1t_ref[...], preferred_element_type=F32).astype(BF16)
    inf = jnp.inf
    k = PK_TOPK
    hk = k // 2
    for h in range(PK_HEADS):
        s1 = jnp.dot(keys_ref[2 * h], qt[(2 * h) * N_KEYS:(2 * h + 1) * N_KEYS],
                     preferred_element_type=F32)
        s2 = jnp.dot(keys_ref[2 * h + 1], qt[(2 * h + 1) * N_KEYS:(2 * h + 2) * N_KEYS],
                     preferred_element_type=F32)
        m1_rep = _top_values(s1)
        m2_rep = _top_values(s2)
        m1 = [m[0:1] for m in m1_rep]
        m2 = [m[0:1] for m in m2_rep]
        sub = lax.broadcasted_iota(jnp.int32, m1_rep[0].shape, 0)

        def stack(reps):
            out = reps[0]
            for r in range(1, hk):
                out = jnp.where(sub == r, reps[r], out)
            return out

        m2a = stack(m2_rep[:hk])
        m2b = stack(m2_rep[hk:])
        m1b = stack(m1_rep[hk:])
        cands = ([m1_rep[0] + m2a, m1_rep[0] + m2b] + [m1_rep[a] + m2a for a in range(1, hk)]
                 + [m1b + m2_rep[0]])
        top = m1[0] + m2[0]
        work = list(cands)
        theta = top
        for it in range(k):
            mx = functools.reduce(jnp.maximum, work)
            theta = jnp.max(mx, axis=0, keepdims=True)
            if it + 1 < k:
                work = [jnp.where(w == theta, -inf, w) for w in work]
        sel = [c >= theta for c in cands]
        z = functools.reduce(
            lambda a, b: a + b,
            [jnp.sum(jnp.where(sl, jnp.exp(c - top), 0.0), axis=0, keepdims=True)
             for sl, c in zip(sel, cands)])
        ones = [jnp.where(sl, 1.0, 0.0) for sl in sel]
        n_sel = [jnp.sum(ones[0] + ones[1], axis=0, keepdims=True)]
        n_sel += [jnp.sum(ones[a + 1], axis=0, keepdims=True) for a in range(1, hk)]
        n_sel_b = ones[hk + 1]
        cnt = jnp.zeros(s1.shape, F32)
        for a in range(hk):
            cnt = jnp.where(s1 == m1[a], n_sel[a], cnt)
        for r in range(k - hk):
            cnt = jnp.where(s1 == m1[hk + r], n_sel_b[r:r + 1], cnt)
        rank = jnp.full(s2.shape, float(k), F32)
        for b in reversed(range(k)):
            rank = jnp.where(s2 >= m2[b], float(b), rank)
        e1_ref[h] = jnp.exp(s1 - m1[0]) / z
        cnt_ref[h] = cnt
        rank_ref[h] = rank.astype(rank_ref.dtype)
        e2_ref[h] = jnp.exp(s2 - m2[0]).astype(e2_ref.dtype)


def _route(x1t, wpqt, keys, tm):
    d, t = x1t.shape
    blk = pl.BlockSpec((PK_HEADS, N_KEYS, tm), lambda i: (0, 0, i))
    out = lambda dt: jax.ShapeDtypeStruct((PK_HEADS, N_KEYS, t), dt)
    return pl.pallas_call(
        _route_kernel,
        grid=(t // tm,),
        in_specs=[pl.BlockSpec((d, tm), lambda i: (0, i)), _resident(wpqt.shape),
                  _resident(keys.shape)],
        out_specs=[blk, blk, blk, blk],
        out_shape=[out(F32), out(F32), out(BF16), out(BF16)],
        name="peer_route",
        compiler_params=pltpu.CompilerParams(dimension_semantics=("parallel",),
                                             vmem_limit_bytes=VMEM_LIMIT_BYTES),
    )(x1t, wpqt, keys)


def _gate_chunk(tile, cols, e1_ref, cnt_ref, rank_ref, e2_ref, h_rd, w_sc, *, te):
    inv_sqrt2 = 1.0 / math.sqrt(2.0)
    n_i = te // N_KEYS
    width = cols.stop - cols.start
    zero = jnp.zeros((), BF16)
    for ii in range(n_i):
        rows = slice(ii * N_KEYS, (ii + 1) * N_KEYS)
        key = tile * n_i + ii
        g = jnp.zeros((N_KEYS, width), BF16)
        for h in range(PK_HEADS):
            cnt_row = cnt_ref[h, pl.ds(key, 1), cols].astype(BF16)
            e1_row = e1_ref[h, pl.ds(key, 1), cols].astype(BF16)
            g = g + e1_row * jnp.where(rank_ref[h, :, cols] < cnt_row, e2_ref[h, :, cols], zero)
        hh = h_rd[rows, cols]
        act = 0.5 * hh * (1.0 + lax.erf(hh * inv_sqrt2))
        w_sc[rows, cols] = act.astype(BF16) * g


def _peer_kernel(x1t_ref, u_ref, vt_ref, e1_ref, cnt_ref, rank_ref, e2_ref, x1_ref, g2_ref, b2_ref,
                 o_ref, h0_sc, h1_sc, w_sc, acc_sc, *, te, tm, n_e, lane_chunk, alpha):
    s = pl.program_id(1)

    @pl.when(s == 0)
    def _():
        acc_sc[...] = jnp.zeros(acc_sc.shape, F32)
        h1_sc[...] = jnp.zeros(h1_sc.shape, F32)

    tile = jnp.maximum(s - 1, 0)

    def step(h_wr, h_rd):
        for c0 in range(0, tm, lane_chunk):
            cols = slice(c0, c0 + lane_chunk)
            h_wr[:, cols] = jnp.dot(u_ref[...], x1t_ref[:, cols], preferred_element_type=F32)
            _gate_chunk(tile, cols, e1_ref, cnt_ref, rank_ref, e2_ref, h_rd, w_sc, te=te)
            acc_sc[:, cols] += jnp.dot(vt_ref[...], w_sc[:, cols], preferred_element_type=F32)

    @pl.when(s % 2 == 0)
    def _():
        step(h0_sc, h1_sc)

    @pl.when(s % 2 == 1)
    def _():
        step(h1_sc, h0_sc)

    @pl.when(s == n_e)
    def _():
        r = alpha * x1_ref[...] + acc_sc[...].T
        o_ref[...] = _layernorm(r, g2_ref[...], b2_ref[...])


def _peer(x1, x1t, u_bf, vt_bf, e1, cnt, rank, e2, ln_g, ln_b, tm, te, alpha):
    t, d = x1.shape
    n_e = u_bf.shape[0] // te
    all_blk = pl.BlockSpec((PK_HEADS, N_KEYS, tm), lambda i, s: (0, 0, i))
    return pl.pallas_call(
        functools.partial(_peer_kernel, te=te, tm=tm, n_e=n_e, lane_chunk=min(tm, 256),
                          alpha=alpha),
        grid=(t // tm, n_e + 1),
        in_specs=[pl.BlockSpec((d, tm), lambda i, s: (0, i)),
                  pl.BlockSpec((te, d), lambda i, s: (jnp.minimum(s, n_e - 1), 0)),
                  pl.BlockSpec((d, te), lambda i, s: (0, jnp.maximum(s - 1, 0))),
                  all_blk, all_blk, all_blk, all_blk,
                  pl.BlockSpec((tm, d), lambda i, s: (i, 0)),
                  _resident(ln_g.shape), _resident(ln_b.shape)],
        out_specs=pl.BlockSpec((tm, d), lambda i, s: (i, 0)),
        out_shape=jax.ShapeDtypeStruct((t, d), F32),
        scratch_shapes=[pltpu.VMEM((te, tm), F32), pltpu.VMEM((te, tm), F32),
                        pltpu.VMEM((te, tm), BF16), pltpu.VMEM((d, tm), F32)],
        name="peer_experts",
        compiler_params=pltpu.CompilerParams(dimension_semantics=("parallel", "arbitrary"),
                                             vmem_limit_bytes=VMEM_LIMIT_BYTES),
    )(x1t, u_bf, vt_bf, e1, cnt, rank, e2, x1, ln_g, ln_b)


def _rope_freq_row():
    inv_freq = ROPE_THETA ** (-jnp.arange(0, ROT_DIM, 2, dtype=F32) / ROT_DIM)
    d = jnp.arange(V_DIM) % HEAD_DIM
    return jnp.where(d < ROT_DIM, inv_freq[d % (ROT_DIM // 2)], 0.0).astype(F32)[None, :]


def kernel(x, positions, w_in, conv_w, conv_b, gate_b, lambda_q1, lambda_k1, lambda_q2, lambda_k2,
           subln_g, w_conv_out, w_attn_out, w_o, ln1_g, ln1_b, w_pq, sub_keys, u_tab, v_tab,
           ln2_g, ln2_b):
    bsz, seq, d = x.shape
    depth = w_in.shape[0]
    t = bsz * seq
    assert d == N_HEADS * V_DIM and sub_keys.shape[1:] == (PK_HEADS, 2, N_KEYS, N_KEYS)
    tm = min(512, seq)
    tq = min(256, seq)
    te = 1024
    heads_per_step = 4
    alpha = (2 * depth) ** 0.25
    row = lambda a: a.reshape(1, -1).astype(F32)
    freq_row = _rope_freq_row()
    pos2 = positions.reshape(t, 1)
    x2 = x.reshape(t, d)
    for l in range(depth):
        lam_init = 0.8 - 0.6 * math.exp(-0.3 * l)
        w_l = w_in[l].astype(BF16)
        w_hbc, w_qkv, w_g = w_l[:, :3 * d], w_l[:, 3 * d:6 * d], w_l[:, 6 * d:]
        q, k, v = _qkv_rope(x2, pos2, freq_row, w_qkv, tm)
        y_attn = _attention(q, k, v, row(lambda_q1[l]), row(lambda_k1[l]), row(lambda_q2[l]),
                            row(lambda_k2[l]), row(subln_g[l]), bsz, seq, tq, heads_per_step,
                            lam_init)
        x1, x1t = _mix(x2, y_attn, w_hbc, w_g, w_conv_out[l].astype(BF16),
                       w_attn_out[l].astype(BF16), w_o[l].astype(BF16), conv_w[l].astype(F32),
                       row(conv_b[l]), row(gate_b[l]), row(ln1_g[l]), row(ln1_b[l]),
                       tm, seq, alpha)
        keys = sub_keys[l].reshape(PK_HEADS * 2, N_KEYS, N_KEYS).astype(BF16)
        e1, cnt, rank, e2 = _route(x1t, w_pq[l].T.astype(BF16), keys, min(256, tm))
        x2 = _peer(x1, x1t, u_tab[l].astype(BF16), v_tab[l].T.astype(BF16), e1, cnt, rank, e2,
                   row(ln2_g[l]), row(ln2_b[l]), tm, te, alpha)
    return x2.reshape(bsz, seq, d)
```

```python
import functools
import math

import jax
import jax.numpy as jnp
from jax import lax
from jax.experimental import pallas as pl
from jax.experimental.pallas import tpu as pltpu

N_HEADS = 8
HEAD_DIM = 64
V_DIM = 2 * HEAD_DIM
ROT_DIM = HEAD_DIM // 4
ROPE_THETA = 500000.0
CONV_K = 3
N_KEYS = 128
PK_HEADS = 8
PK_TOPK = 16
LN_EPS = 1e-5
CONV_HALO = 16

VMEM_LIMIT_BYTES = 52 * 1024 * 1024

F32 = jnp.float32
BF16 = jnp.bfloat16


def _resident(shape):
    nd = len(shape)
    return pl.BlockSpec(shape, lambda *_: (0,) * nd, pipeline_mode=pl.Buffered(1))


def _layernorm(r, g, b):
    mu = jnp.mean(r, axis=-1, keepdims=True)
    c = r - mu
    var = jnp.mean(c * c, axis=-1, keepdims=True)
    return c * lax.rsqrt(var + LN_EPS) * g + b


def _qkv_rope_kernel(x_ref, pos_ref, freq_ref, w_ref, q_ref, k_ref, v_ref):
    d_model = x_ref.shape[1]
    proj = jnp.dot(x_ref[...].astype(BF16), w_ref[...], preferred_element_type=F32)
    ang = pos_ref[...].astype(F32) * freq_ref[...]
    cos = jnp.cos(ang)
    sin = jnp.sin(ang)
    d = lax.broadcasted_iota(jnp.int32, ang.shape, 1) & (HEAD_DIM - 1)
    half = ROT_DIM // 2
    lo = d < half
    hi = jnp.logical_and(d >= half, d < ROT_DIM)
    c_tab = jnp.where(d < ROT_DIM, cos, 1.0)
    s_lo = jnp.where(lo, -sin, 0.0)
    s_hi = jnp.where(hi, sin, 0.0)
    scale = HEAD_DIM ** -0.5 * math.log2(math.e)
    for h in range(N_HEADS):
        for off, ref, mul in ((0, q_ref, scale), (d_model, k_ref, 1.0)):
            t = proj[:, off + h * V_DIM: off + (h + 1) * V_DIM]
            up = jnp.where(lo, pltpu.roll(t, V_DIM - half, 1), 0.0)
            dn = jnp.where(hi, pltpu.roll(t, half, 1), 0.0)
            r = t * c_tab + up * s_lo + dn * s_hi
            ref[:, h * V_DIM:(h + 1) * V_DIM] = (r * mul).astype(BF16)
    v_ref[...] = proj[:, 2 * d_model:].astype(BF16)


def _qkv_rope(x2, pos2, freq_row, w_qkv, tm):
    t, d = x2.shape
    out = jax.ShapeDtypeStruct((t, d), BF16)
    tok = pl.BlockSpec((tm, d), lambda i: (i, 0))
    return pl.pallas_call(
        _qkv_rope_kernel,
        grid=(t // tm,),
        in_specs=[tok, pl.BlockSpec((tm, 1), lambda i: (i, 0)), _resident(freq_row.shape),
                  _resident(w_qkv.shape)],
        out_specs=[tok, tok, tok],
        out_shape=[out, out, out],
        name="qkv_rope",
        compiler_params=pltpu.CompilerParams(dimension_semantics=("parallel",),
                                             vmem_limit_bytes=VMEM_LIMIT_BYTES),
    )(x2, pos2, freq_row, w_qkv)


def _attn_tile(q_ref, k_ref, v_ref, g_ref, o_ref, lam, *, n, tq, heads, lam_init):
    n_tiles = n // V_DIM
    diag0 = (n - tq) // V_DIM
    row = lax.broadcasted_iota(jnp.int32, (2 * tq, V_DIM), 0)
    qpos = jnp.where(row >= tq, row - tq, row)
    lane = lax.broadcasted_iota(jnp.int32, (2 * tq, V_DIM), 1)
    lane_q = lax.broadcasted_iota(jnp.int32, (tq, V_DIM), 1)
    cols = [slice(h * V_DIM, (h + 1) * V_DIM) for h in range(heads)]
    scores = []
    for c in cols:
        q = q_ref[:, c]
        zero = jnp.zeros_like(q)
        qq = jnp.concatenate([jnp.where(lane_q < HEAD_DIM, q, zero),
                              jnp.where(lane_q >= HEAD_DIM, q, zero)], axis=0)
        scores.append(lax.dot_general(qq, k_ref[0:n, c], (((1,), (1,)), ((), ())),
                                      preferred_element_type=F32))
    tiles = []
    for s in scores:
        ts = []
        for j in range(n_tiles):
            t = s[:, j * V_DIM:(j + 1) * V_DIM]
            if j >= diag0:
                t = jnp.where(lane + (j - diag0) * V_DIM <= qpos, t, -jnp.inf)
            ts.append(t)
        tiles.append(ts)
    mx = [jnp.max(functools.reduce(jnp.maximum, ts), axis=-1, keepdims=True) for ts in tiles]
    p = [[jnp.exp2(t - m) for t in ts] for ts, m in zip(tiles, mx)]
    r = [1.0 / jnp.sum(functools.reduce(lambda a, b: a + b, ps), axis=-1, keepdims=True)
         for ps in p]
    outs = []
    for ps, rh, c in zip(p, r, cols):
        r1 = rh[:tq]
        r2 = lam * rh[tq:]
        a = jnp.concatenate([(pt[:tq] * r1 - pt[tq:] * r2).astype(BF16) for pt in ps], axis=1)
        outs.append(jnp.dot(a, v_ref[0:n, c], preferred_element_type=F32))
    for o, c in zip(outs, cols):
        ms = jnp.mean(o * o, axis=-1, keepdims=True)
        y = o * lax.rsqrt(ms + LN_EPS) * g_ref[...] * (1.0 - lam_init)
        o_ref[:, c] = y.astype(o_ref.dtype)


def _attn_kernel(lq1_ref, lk1_ref, lq2_ref, lk2_ref, g_ref, q_ref, k_ref, v_ref, o_ref,
                 *, tq, nq, heads, lam_init):
    i = pl.program_id(2)
    lam = (jnp.exp(jnp.sum(lq1_ref[...] * lk1_ref[...], axis=-1, keepdims=True))
           - jnp.exp(jnp.sum(lq2_ref[...] * lk2_ref[...], axis=-1, keepdims=True)) + lam_init)
    for nb in range(1, nq + 1):
        @pl.when(i == nb - 1)
        def _(nb=nb):
            _attn_tile(q_ref, k_ref, v_ref, g_ref, o_ref, lam, n=nb * tq, tq=tq, heads=heads,
                       lam_init=lam_init)


def _attention(q, k, v, lq1, lk1, lq2, lk2, subln_g, bsz, seq, tq, heads, lam_init):
    t, d = q.shape
    nq = seq // tq
    width = heads * V_DIM
    vec = lambda a: _resident(a.shape)
    return pl.pallas_call(
        functools.partial(_attn_kernel, tq=tq, nq=nq, heads=heads, lam_init=lam_init),
        grid=(bsz, N_HEADS // heads, nq),
        in_specs=[vec(lq1), vec(lk1), vec(lq2), vec(lk2), vec(subln_g),
                  pl.BlockSpec((tq, width), lambda b, h, i: (b * nq + i, h)),
                  pl.BlockSpec((seq, width), lambda b, h, i: (b, h)),
                  pl.BlockSpec((seq, width), lambda b, h, i: (b, h))],
        out_specs=pl.BlockSpec((tq, width), lambda b, h, i: (b * nq + i, h)),
        out_shape=jax.ShapeDtypeStruct((t, d), BF16),
        name="diff_attn",
        compiler_params=pltpu.CompilerParams(
            dimension_semantics=("parallel", "parallel", "arbitrary"),
            vmem_limit_bytes=VMEM_LIMIT_BYTES),
    )(lq1, lk1, lq2, lk2, subln_g, q, k, v)


def _mix_kernel(xp_ref, x_ref, ya_ref, w_hbc_ref, w_g_ref, w_co_ref, w_ao_ref, w_o_ref,
                cw_ref, cb_ref, gb_ref, g1_ref, b1_ref, x1_ref, x1t_ref, z_sc,
                *, tm, tiles_per_seq, alpha):
    i = pl.program_id(0)
    d = x_ref.shape[1]
    x = x_ref[...]
    xcat = jnp.concatenate([xp_ref[...], x], axis=0).astype(BF16)
    hbc = jnp.dot(xcat, w_hbc_ref[...], preferred_element_type=F32)
    z = hbc[:, :d] * hbc[:, 2 * d:]
    row = lax.broadcasted_iota(jnp.int32, z.shape, 0)
    seq_start = (i % tiles_per_seq) == 0
    z_sc[...] = jnp.where(jnp.logical_and(row < CONV_HALO, seq_start), 0.0, z)
    cw = cw_ref[...]
    y = cb_ref[...] + cw[CONV_K - 1:CONV_K] * z_sc[pl.ds(CONV_HALO, tm), :]
    for j in range(CONV_K - 1):
        y = y + cw[j:j + 1] * z_sc[pl.ds(CONV_HALO - (CONV_K - 1) + j, tm), :]
    y_conv = hbc[CONV_HALO:, d:2 * d] * y
    c_out = jnp.dot(y_conv.astype(BF16), w_co_ref[...], preferred_element_type=F32)
    a_out = jnp.dot(ya_ref[...], w_ao_ref[...], preferred_element_type=F32)
    gates = jax.nn.sigmoid(
        jnp.dot(xcat[CONV_HALO:], w_g_ref[...], preferred_element_type=F32) + gb_ref[...])
    merged = gates[:, :d] * c_out + gates[:, d:] * a_out
    r = alpha * x + jnp.dot(merged.astype(BF16), w_o_ref[...], preferred_element_type=F32)
    x1 = _layernorm(r, g1_ref[...], b1_ref[...])
    x1_ref[...] = x1
    x1t_ref[...] = x1.T.astype(BF16)


def _mix(x2, y_attn, w_hbc, w_g, w_co, w_ao, w_o, conv_w, conv_b, gate_b, ln_g, ln_b,
         tm, seq, alpha):
    t, d = x2.shape
    halo_blocks = tm // CONV_HALO
    tok = pl.BlockSpec((tm, d), lambda i: (i, 0))
    prev = pl.BlockSpec((CONV_HALO, d), lambda i: (jnp.maximum(i * halo_blocks - 1, 0), 0))
    res = lambda a: _resident(a.shape)
    return pl.pallas_call(
        functools.partial(_mix_kernel, tm=tm, tiles_per_seq=seq // tm, alpha=alpha),
        grid=(t // tm,),
        in_specs=[prev, tok, tok, res(w_hbc), res(w_g), res(w_co), res(w_ao), res(w_o),
                  res(conv_w), res(conv_b), res(gate_b), res(ln_g), res(ln_b)],
        out_specs=[tok, pl.BlockSpec((d, tm), lambda i: (0, i))],
        out_shape=[jax.ShapeDtypeStruct((t, d), F32), jax.ShapeDtypeStruct((d, t), BF16)],
        scratch_shapes=[pltpu.VMEM((CONV_HALO + tm, d), F32)],
        name="conv_merge_ln1",
        compiler_params=pltpu.CompilerParams(dimension_semantics=("parallel",),
                                             vmem_limit_bytes=VMEM_LIMIT_BYTES),
    )(x2, x2, y_attn, w_hbc, w_g, w_co, w_ao, w_o, conv_w, conv_b, gate_b, ln_g, ln_b)


def _sort_network(n):
    pairs = []

    def merge(lo, hi, r):
        step = r * 2
        if step < hi - lo:
            merge(lo, hi, step)
            merge(lo + r, hi, step)
            pairs.extend((i, i + r) for i in range(lo + r, hi - r, step))
        else:
            pairs.append((lo, lo + r))

    def sort(lo, hi):
        if hi - lo >= 1:
            mid = lo + (hi - lo) // 2
            sort(lo, mid)
            sort(mid + 1, hi)
            merge(lo, hi, 1)

    sort(0, n - 1)
    return tuple(pairs)


SUBLANES = 8
_SORT_PAIRS = _sort_network(PK_TOPK)
_BITONIC_PAIRS = tuple((i, i + d) for d in (8, 4, 2, 1) for i in range(PK_TOPK) if not i & d)


def _compare_exchange(a, pairs):
    for i, j in pairs:
        a[i], a[j] = jnp.maximum(a[i], a[j]), jnp.minimum(a[i], a[j])


def _top_values(s):
    k = PK_TOPK
    assert s.shape[0] == k * SUBLANES
    a = [s[SUBLANES * v:SUBLANES * (v + 1)] for v in range(k)]
    _compare_exchange(a, _SORT_PAIRS)
    shift = SUBLANES // 2
    while shift:
        b = [pltpu.roll(x, shift, 0) for x in a]
        a = [jnp.maximum(a[i], b[k - 1 - i]) for i in range(k)]
        _compare_exchange(a, _BITONIC_PAIRS)
        shift //= 2
    return a


def _route_kernel(x1t_ref, wpqt_ref, keys_ref, e1_ref, cnt_ref, rank_ref, e2_ref):
    qt = jnp.dot(wpqt_ref[...], x1t_ref[...], preferred_element_type=F32).astype(BF16)
    inf = jnp.inf
    k = PK_TOPK
    hk = k // 2
    for h in range(PK_HEADS):
        s1 = jnp.dot(keys_ref[2 * h], qt[(2 * h) * N_KEYS:(2 * h + 1) * N_KEYS],
                     preferred_element_type=F32)
        s2 = jnp.dot(keys_ref[2 * h + 1], qt[(2 * h + 1) * N_KEYS:(2 * h + 2) * N_KEYS],
                     preferred_element_type=F32)
        m1_rep = _top_values(s1)
        m2_rep = _top_values(s2)
        m1 = [m[0:1] for m in m1_rep]
        m2 = [m[0:1] for m in m2_rep]
        sub = lax.broadcasted_iota(jnp.int32, m1_rep[0].shape, 0)

        def stack(reps):
            out = reps[0]
            for r in range(1, hk):
                out = jnp.where(sub == r, reps[r], out)
            return out

        m2a = stack(m2_rep[:hk])
        m2b = stack(m2_rep[hk:])
        m1b = stack(m1_rep[hk:])
        cands = ([m1_rep[0] + m2a, m1_rep[0] + m2b] + [m1_rep[a] + m2a for a in range(1, hk)]
                 + [m1b + m2_rep[0]])
        top = m1[0] + m2[0]
        work = list(cands)
        theta = top
        for it in range(k):
            mx = functools.reduce(jnp.maximum, work)
            theta = jnp.max(mx, axis=0, keepdims=True)
            if it + 1 < k:
                work = [jnp.where(w == theta, -inf, w) for w in work]
        sel = [c >= theta for c in cands]
        z = functools.reduce(
            lambda a, b: a + b,
            [jnp.sum(jnp.where(sl, jnp.exp(c - top), 0.0), axis=0, keepdims=True)
             for sl, c in zip(sel, cands)])
        ones = [jnp.where(sl, 1.0, 0.0) for sl in sel]
        n_sel = [jnp.sum(ones[0] + ones[1], axis=0, keepdims=True)]
        n_sel += [jnp.sum(ones[a + 1], axis=0, keepdims=True) for a in range(1, hk)]
        n_sel_b = ones[hk + 1]
        cnt = jnp.zeros(s1.shape, F32)
        for a in range(hk):
            cnt = jnp.where(s1 == m1[a], n_sel[a], cnt)
        for r in range(k - hk):
            cnt = jnp.where(s1 == m1[hk + r], n_sel_b[r:r + 1], cnt)
        rank = jnp.full(s2.shape, float(k), F32)
        for b in reversed(range(k)):
            rank = jnp.where(s2 >= m2[b], float(b), rank)
        e1_ref[h] = jnp.exp(s1 - m1[0]) * (0.5 / z)
        cnt_ref[h] = cnt
        rank_ref[h] = rank.astype(rank_ref.dtype)
        e2_ref[h] = jnp.exp(s2 - m2[0]).astype(e2_ref.dtype)


def _route(x1t, wpqt, keys, tm):
    d, t = x1t.shape
    blk = pl.BlockSpec((PK_HEADS, N_KEYS, tm), lambda i: (0, 0, i))
    out = lambda dt: jax.ShapeDtypeStruct((PK_HEADS, N_KEYS, t), dt)
    return pl.pallas_call(
        _route_kernel,
        grid=(t // tm,),
        in_specs=[pl.BlockSpec((d, tm), lambda i: (0, i)), _resident(wpqt.shape),
                  _resident(keys.shape)],
        out_specs=[blk, blk, blk, blk],
        out_shape=[out(F32), out(F32), out(BF16), out(BF16)],
        name="peer_route",
        compiler_params=pltpu.CompilerParams(dimension_semantics=("parallel",),
                                             vmem_limit_bytes=VMEM_LIMIT_BYTES),
    )(x1t, wpqt, keys)


def _gate_chunk(key0, first_keys, cols, e1_ref, cnt_ref, rank_ref, e2_ref, h_rd, w_wr):
    inv_sqrt2 = 1.0 / math.sqrt(2.0)
    width = cols.stop - cols.start
    zero = jnp.zeros((), BF16)
    for ii in first_keys:
        rows = slice(ii * N_KEYS, (ii + 1) * N_KEYS)
        key = key0 + ii
        g = jnp.zeros((N_KEYS, width), BF16)
        for h in range(PK_HEADS):
            pack = (2 * SUBLANES, width)
            cnt_row = jnp.broadcast_to(cnt_ref[h, pl.ds(key, 1), cols], pack).astype(BF16)
            e1_row = jnp.broadcast_to(e1_ref[h, pl.ds(key, 1), cols], pack).astype(BF16)
            cnt_row = pltpu.repeat(cnt_row, N_KEYS // (2 * SUBLANES), 0)
            e1_row = pltpu.repeat(e1_row, N_KEYS // (2 * SUBLANES), 0)
            g = g + e1_row * jnp.where(rank_ref[h, :, cols] < cnt_row, e2_ref[h, :, cols], zero)
        hh = h_rd[rows, cols].astype(BF16)
        w_wr[rows, cols] = (hh * g) * (1.0 + lax.erf(hh * inv_sqrt2))


def _peer_kernel(x1t_ref, u_ref, vt_ref, e1_ref, cnt_ref, rank_ref, e2_ref, x1_ref, g2_ref, b2_ref,
                 o_ref, h0_sc, h1_sc, w_sc, acc_sc, *, te, tm, n_e, n_pairs, lane_chunk, alpha):
    g = pl.program_id(0)
    n_i = te // N_KEYS
    e_prev = jnp.clip(g - 1, 0, n_pairs - 1) % n_e
    key0 = e_prev * n_i
    first_keys = range(n_i)

    @pl.when(g == 0)
    def _():
        h1_sc[...] = jnp.zeros(h1_sc.shape, F32)

    @pl.when(e_prev == 0)
    def _():
        acc_sc[...] = jnp.zeros(acc_sc.shape, F32)

    def step(h_wr, h_rd):
        for c0 in range(0, tm, lane_chunk):
            cols = slice(c0, c0 + lane_chunk)
            h_wr[:, cols] = jnp.dot(u_ref[...], x1t_ref[:, cols], preferred_element_type=F32)
            _gate_chunk(key0, first_keys, cols, e1_ref, cnt_ref, rank_ref, e2_ref, h_rd, w_sc)
            acc_sc[:, cols] += jnp.dot(vt_ref[...], w_sc[:, cols], preferred_element_type=F32)

    @pl.when(g % 2 == 0)
    def _():
        step(h0_sc, h1_sc)

    @pl.when(g % 2 == 1)
    def _():
        step(h1_sc, h0_sc)

    @pl.when(jnp.logical_and(e_prev == n_e - 1, g >= 1))
    def _():
        r = alpha * x1_ref[...] + acc_sc[...].T
        o_ref[...] = _layernorm(r, g2_ref[...], b2_ref[...])


def _peer(x1, x1t, u_bf, vt_bf, e1, cnt, rank, e2, ln_g, ln_b, tm, te, alpha):
    t, d = x1.shape
    n_e = u_bf.shape[0] // te
    n_pairs = (t // tm) * n_e
    pair = lambda g, lag: jnp.clip(g - lag, 0, n_pairs - 1)
    gate_blk = pl.BlockSpec((PK_HEADS, N_KEYS, tm), lambda g: (0, 0, pair(g, 1) // n_e))
    return pl.pallas_call(
        functools.partial(_peer_kernel, te=te, tm=tm, n_e=n_e, n_pairs=n_pairs,
                          lane_chunk=min(tm, 256), alpha=alpha),
        grid=(n_pairs + 1,),
        in_specs=[pl.BlockSpec((d, tm), lambda g: (0, pair(g, 0) // n_e)),
                  pl.BlockSpec((te, d), lambda g: (pair(g, 0) % n_e, 0)),
                  pl.BlockSpec((d, te), lambda g: (0, pair(g, 1) % n_e)),
                  gate_blk, gate_blk, gate_blk, gate_blk,
                  pl.BlockSpec((tm, d), lambda g: (pair(g, 1) // n_e, 0)),
                  _resident(ln_g.shape), _resident(ln_b.shape)],
        out_specs=pl.BlockSpec((tm, d), lambda g: (pair(g, 1) // n_e, 0)),
        out_shape=jax.ShapeDtypeStruct((t, d), F32),
        scratch_shapes=[pltpu.VMEM((te, tm), F32), pltpu.VMEM((te, tm), F32),
                        pltpu.VMEM((te, tm), BF16), pltpu.VMEM((d, tm), F32)],
        name="peer_experts",
        compiler_params=pltpu.CompilerParams(dimension_semantics=("arbitrary",),
                                             vmem_limit_bytes=VMEM_LIMIT_BYTES),
    )(x1t, u_bf, vt_bf, e1, cnt, rank, e2, x1, ln_g, ln_b)


def _rope_freq_row():
    inv_freq = ROPE_THETA ** (-jnp.arange(0, ROT_DIM, 2, dtype=F32) / ROT_DIM)
    d = jnp.arange(V_DIM) % HEAD_DIM
    return jnp.where(d < ROT_DIM, inv_freq[d % (ROT_DIM // 2)], 0.0).astype(F32)[None, :]


def kernel(x, positions, w_in, conv_w, conv_b, gate_b, lambda_q1, lambda_k1, lambda_q2, lambda_k2,
           subln_g, w_conv_out, w_attn_out, w_o, ln1_g, ln1_b, w_pq, sub_keys, u_tab, v_tab,
           ln2_g, ln2_b):
    bsz, seq, d = x.shape
    depth = w_in.shape[0]
    t = bsz * seq
    assert d == N_HEADS * V_DIM and sub_keys.shape[1:] == (PK_HEADS, 2, N_KEYS, N_KEYS)
    tm = min(512, seq)
    tq = min(256, seq)
    te = 1024
    heads_per_step = 4
    alpha = (2 * depth) ** 0.25
    row = lambda a: a.reshape(1, -1).astype(F32)
    freq_row = _rope_freq_row()
    pos2 = positions.reshape(t, 1)
    x2 = x.reshape(t, d)
    for l in range(depth):
        lam_init = 0.8 - 0.6 * math.exp(-0.3 * l)
        w_l = w_in[l].astype(BF16)
        w_hbc, w_qkv, w_g = w_l[:, :3 * d], w_l[:, 3 * d:6 * d], w_l[:, 6 * d:]
        q, k, v = _qkv_rope(x2, pos2, freq_row, w_qkv, tm)
        y_attn = _attention(q, k, v, row(lambda_q1[l]), row(lambda_k1[l]), row(lambda_q2[l]),
                            row(lambda_k2[l]), row(subln_g[l]), bsz, seq, tq, heads_per_step,
                            lam_init)
        x1, x1t = _mix(x2, y_attn, w_hbc, w_g, w_conv_out[l].astype(BF16),
                       w_attn_out[l].astype(BF16), w_o[l].astype(BF16), conv_w[l].astype(F32),
                       row(conv_b[l]), row(gate_b[l]), row(ln1_g[l]), row(ln1_b[l]),
                       tm, seq, alpha)
        keys = sub_keys[l].reshape(PK_HEADS * 2, N_KEYS, N_KEYS).astype(BF16)
        e1, cnt, rank, e2 = _route(x1t, w_pq[l].T.astype(BF16), keys, min(256, tm))
        x2 = _peer(x1, x1t, u_tab[l].astype(BF16), v_tab[l].T.astype(BF16), e1, cnt, rank, e2,
                   row(ln2_g[l]), row(ln2_b[l]), tm, te, alpha)
    return x2.reshape(bsz, seq, d)
```

```python
import functools
import math

import jax
import jax.numpy as jnp
from jax import lax
from jax.experimental import pallas as pl
from jax.experimental.pallas import tpu as pltpu

N_HEADS = 8
HEAD_DIM = 64
V_DIM = 2 * HEAD_DIM
ROT_DIM = HEAD_DIM // 4
ROPE_THETA = 500000.0
CONV_K = 3
N_KEYS = 128
PK_HEADS = 8
PK_TOPK = 16
LN_EPS = 1e-5
CONV_HALO = 16

VMEM_LIMIT_BYTES = 52 * 1024 * 1024

F32 = jnp.float32
BF16 = jnp.bfloat16


def _resident(shape):
    nd = len(shape)
    return pl.BlockSpec(shape, lambda *_: (0,) * nd, pipeline_mode=pl.Buffered(1))


def _layernorm(r, g, b):
    mu = jnp.mean(r, axis=-1, keepdims=True)
    c = r - mu
    var = jnp.mean(c * c, axis=-1, keepdims=True)
    return c * lax.rsqrt(var + LN_EPS) * g + b


def _qkv_rope_kernel(x_ref, pos_ref, freq_ref, w_ref, q_ref, k_ref, v_ref):
    d_model = x_ref.shape[1]
    proj = jnp.dot(x_ref[...].astype(BF16), w_ref[...], preferred_element_type=F32)
    ang = pos_ref[...].astype(F32) * freq_ref[...]
    cos = jnp.cos(ang)
    sin = jnp.sin(ang)
    d = lax.broadcasted_iota(jnp.int32, ang.shape, 1) & (HEAD_DIM - 1)
    half = ROT_DIM // 2
    lo = d < half
    hi = jnp.logical_and(d >= half, d < ROT_DIM)
    c_tab = jnp.where(d < ROT_DIM, cos, 1.0)
    s_lo = jnp.where(lo, -sin, 0.0)
    s_hi = jnp.where(hi, sin, 0.0)
    scale = HEAD_DIM ** -0.5 * math.log2(math.e)
    for h in range(N_HEADS):
        for off, ref, mul in ((0, q_ref, scale), (d_model, k_ref, 1.0)):
            t = proj[:, off + h * V_DIM: off + (h + 1) * V_DIM]
            up = jnp.where(lo, pltpu.roll(t, V_DIM - half, 1), 0.0)
            dn = jnp.where(hi, pltpu.roll(t, half, 1), 0.0)
            r = t * c_tab + up * s_lo + dn * s_hi
            ref[:, h * V_DIM:(h + 1) * V_DIM] = (r * mul).astype(BF16)
    v_ref[...] = proj[:, 2 * d_model:].astype(BF16)


def _qkv_rope(x2, pos2, freq_row, w_qkv, tm):
    t, d = x2.shape
    out = jax.ShapeDtypeStruct((t, d), BF16)
    tok = pl.BlockSpec((tm, d), lambda i: (i, 0))
    return pl.pallas_call(
        _qkv_rope_kernel,
        grid=(t // tm,),
        in_specs=[tok, pl.BlockSpec((tm, 1), lambda i: (i, 0)), _resident(freq_row.shape),
                  _resident(w_qkv.shape)],
        out_specs=[tok, tok, tok],
        out_shape=[out, out, out],
        name="qkv_rope",
        compiler_params=pltpu.CompilerParams(dimension_semantics=("parallel",),
                                             vmem_limit_bytes=VMEM_LIMIT_BYTES),
    )(x2, pos2, freq_row, w_qkv)


def _attn_tile(q_ref, k_ref, v_ref, g_ref, o_ref, lam, *, n, tq, heads, lam_init):
    n_tiles = n // V_DIM
    diag0 = (n - tq) // V_DIM
    row = lax.broadcasted_iota(jnp.int32, (2 * tq, V_DIM), 0)
    qpos = jnp.where(row >= tq, row - tq, row)
    lane = lax.broadcasted_iota(jnp.int32, (2 * tq, V_DIM), 1)
    lane_q = lax.broadcasted_iota(jnp.int32, (tq, V_DIM), 1)
    cols = [slice(h * V_DIM, (h + 1) * V_DIM) for h in range(heads)]
    scores = []
    for c in cols:
        q = q_ref[:, c]
        zero = jnp.zeros_like(q)
        qq = jnp.concatenate([jnp.where(lane_q < HEAD_DIM, q, zero),
                              jnp.where(lane_q >= HEAD_DIM, q, zero)], axis=0)
        scores.append(lax.dot_general(qq, k_ref[0:n, c], (((1,), (1,)), ((), ())),
                                      preferred_element_type=F32))
    tiles = []
    for s in scores:
        ts = []
        for j in range(n_tiles):
            t = s[:, j * V_DIM:(j + 1) * V_DIM]
            if j >= diag0:
                t = jnp.where(lane + (j - diag0) * V_DIM <= qpos, t, -jnp.inf)
            ts.append(t)
        tiles.append(ts)
    mx = [jnp.max(functools.reduce(jnp.maximum, ts), axis=-1, keepdims=True) for ts in tiles]
    p = [[jnp.exp2(t - m) for t in ts] for ts, m in zip(tiles, mx)]
    r = [1.0 / jnp.sum(functools.reduce(lambda a, b: a + b, ps), axis=-1, keepdims=True)
         for ps in p]
    outs = []
    for ps, rh, c in zip(p, r, cols):
        r1 = rh[:tq]
        r2 = lam * rh[tq:]
        a = jnp.concatenate([(pt[:tq] * r1 - pt[tq:] * r2).astype(BF16) for pt in ps], axis=1)
        outs.append(jnp.dot(a, v_ref[0:n, c], preferred_element_type=F32))
    for o, c in zip(outs, cols):
        ms = jnp.mean(o * o, axis=-1, keepdims=True)
        y = o * lax.rsqrt(ms + LN_EPS) * g_ref[...] * (1.0 - lam_init)
        o_ref[:, c] = y.astype(o_ref.dtype)


def _attn_kernel(lq1_ref, lk1_ref, lq2_ref, lk2_ref, g_ref, q_ref, k_ref, v_ref, o_ref,
                 *, tq, nq, heads, lam_init):
    i = pl.program_id(2)
    lam = (jnp.exp(jnp.sum(lq1_ref[...] * lk1_ref[...], axis=-1, keepdims=True))
           - jnp.exp(jnp.sum(lq2_ref[...] * lk2_ref[...], axis=-1, keepdims=True)) + lam_init)
    for nb in range(1, nq + 1):
        @pl.when(i == nb - 1)
        def _(nb=nb):
            _attn_tile(q_ref, k_ref, v_ref, g_ref, o_ref, lam, n=nb * tq, tq=tq, heads=heads,
                       lam_init=lam_init)


def _attention(q, k, v, lq1, lk1, lq2, lk2, subln_g, bsz, seq, tq, heads, lam_init):
    t, d = q.shape
    nq = seq // tq
    width = heads * V_DIM
    vec = lambda a: _resident(a.shape)
    return pl.pallas_call(
        functools.partial(_attn_kernel, tq=tq, nq=nq, heads=heads, lam_init=lam_init),
        grid=(bsz, N_HEADS // heads, nq),
        in_specs=[vec(lq1), vec(lk1), vec(lq2), vec(lk2), vec(subln_g),
                  pl.BlockSpec((tq, width), lambda b, h, i: (b * nq + i, h)),
                  pl.BlockSpec((seq, width), lambda b, h, i: (b, h)),
                  pl.BlockSpec((seq, width), lambda b, h, i: (b, h))],
        out_specs=pl.BlockSpec((tq, width), lambda b, h, i: (b * nq + i, h)),
        out_shape=jax.ShapeDtypeStruct((t, d), BF16),
        name="diff_attn",
        compiler_params=pltpu.CompilerParams(
            dimension_semantics=("parallel", "parallel", "arbitrary"),
            vmem_limit_bytes=VMEM_LIMIT_BYTES),
    )(lq1, lk1, lq2, lk2, subln_g, q, k, v)


def _mix_kernel(xp_ref, x_ref, ya_ref, w_hbc_ref, w_g_ref, w_co_ref, w_ao_ref, w_o_ref,
                cw_ref, cb_ref, gb_ref, g1_ref, b1_ref, x1_ref, x1t_ref, z_sc,
                *, tm, tiles_per_seq, alpha):
    i = pl.program_id(0)
    d = x_ref.shape[1]
    x = x_ref[...]
    xcat = jnp.concatenate([xp_ref[...], x], axis=0).astype(BF16)
    hbc = jnp.dot(xcat, w_hbc_ref[...], preferred_element_type=F32)
    z = hbc[:, :d] * hbc[:, 2 * d:]
    row = lax.broadcasted_iota(jnp.int32, z.shape, 0)
    seq_start = (i % tiles_per_seq) == 0
    z_sc[...] = jnp.where(jnp.logical_and(row < CONV_HALO, seq_start), 0.0, z)
    cw = cw_ref[...]
    y = cb_ref[...] + cw[CONV_K - 1:CONV_K] * z_sc[pl.ds(CONV_HALO, tm), :]
    for j in range(CONV_K - 1):
        y = y + cw[j:j + 1] * z_sc[pl.ds(CONV_HALO - (CONV_K - 1) + j, tm), :]
    y_conv = hbc[CONV_HALO:, d:2 * d] * y
    c_out = jnp.dot(y_conv.astype(BF16), w_co_ref[...], preferred_element_type=F32)
    a_out = jnp.dot(ya_ref[...], w_ao_ref[...], preferred_element_type=F32)
    gates = jax.nn.sigmoid(
        jnp.dot(xcat[CONV_HALO:], w_g_ref[...], preferred_element_type=F32) + gb_ref[...])
    merged = gates[:, :d] * c_out + gates[:, d:] * a_out
    r = alpha * x + jnp.dot(merged.astype(BF16), w_o_ref[...], preferred_element_type=F32)
    x1 = _layernorm(r, g1_ref[...], b1_ref[...])
    x1_ref[...] = x1
    x1t_ref[...] = x1.T.astype(BF16)


def _mix(x2, y_attn, w_hbc, w_g, w_co, w_ao, w_o, conv_w, conv_b, gate_b, ln_g, ln_b,
         tm, seq, alpha):
    t, d = x2.shape
    halo_blocks = tm // CONV_HALO
    tok = pl.BlockSpec((tm, d), lambda i: (i, 0))
    prev = pl.BlockSpec((CONV_HALO, d), lambda i: (jnp.maximum(i * halo_blocks - 1, 0), 0))
    res = lambda a: _resident(a.shape)
    return pl.pallas_call(
        functools.partial(_mix_kernel, tm=tm, tiles_per_seq=seq // tm, alpha=alpha),
        grid=(t // tm,),
        in_specs=[prev, tok, tok, res(w_hbc), res(w_g), res(w_co), res(w_ao), res(w_o),
                  res(conv_w), res(conv_b), res(gate_b), res(ln_g), res(ln_b)],
        out_specs=[tok, pl.BlockSpec((d, tm), lambda i: (0, i))],
        out_shape=[jax.ShapeDtypeStruct((t, d), F32), jax.ShapeDtypeStruct((d, t), BF16)],
        scratch_shapes=[pltpu.VMEM((CONV_HALO + tm, d), F32)],
        name="conv_merge_ln1",
        compiler_params=pltpu.CompilerParams(dimension_semantics=("parallel",),
                                             vmem_limit_bytes=VMEM_LIMIT_BYTES),
    )(x2, x2, y_attn, w_hbc, w_g, w_co, w_ao, w_o, conv_w, conv_b, gate_b, ln_g, ln_b)


def _sort_network(n):
    pairs = []

    def merge(lo, hi, r):
        step = r * 2
        if step < hi - lo:
            merge(lo, hi, step)
            merge(lo + r, hi, step)
            pairs.extend((i, i + r) for i in range(lo + r, hi - r, step))
        else:
            pairs.append((lo, lo + r))

    def sort(lo, hi):
        if hi - lo >= 1:
            mid = lo + (hi - lo) // 2
            sort(lo, mid)
            sort(mid + 1, hi)
            merge(lo, hi, 1)

    sort(0, n - 1)
    return tuple(pairs)


SUBLANES = 8
_SORT_PAIRS = _sort_network(PK_TOPK)
_BITONIC_PAIRS = tuple((i, i + d) for d in (8, 4, 2, 1) for i in range(PK_TOPK) if not i & d)


def _compare_exchange(a, pairs):
    for i, j in pairs:
        a[i], a[j] = jnp.maximum(a[i], a[j]), jnp.minimum(a[i], a[j])


def _top_values(s):
    k = PK_TOPK
    assert s.shape[0] == k * SUBLANES
    a = [s[SUBLANES * v:SUBLANES * (v + 1)] for v in range(k)]
    _compare_exchange(a, _SORT_PAIRS)
    shift = SUBLANES // 2
    while shift:
        b = [pltpu.roll(x, shift, 0) for x in a]
        a = [jnp.maximum(a[i], b[k - 1 - i]) for i in range(k)]
        _compare_exchange(a, _BITONIC_PAIRS)
        shift //= 2
    return a


def _route_kernel(x1t_ref, wpqt_ref, keys_ref, e1_ref, cnt_ref, rank_ref, e2_ref):
    qt = jnp.dot(wpqt_ref[...], x1t_ref[...], preferred_element_type=F32).astype(BF16)
    inf = jnp.inf
    k = PK_TOPK
    hk = k // 2
    for h in range(PK_HEADS):
        s1 = jnp.dot(keys_ref[2 * h], qt[(2 * h) * N_KEYS:(2 * h + 1) * N_KEYS],
                     preferred_element_type=F32)
        s2 = jnp.dot(keys_ref[2 * h + 1], qt[(2 * h + 1) * N_KEYS:(2 * h + 2) * N_KEYS],
                     preferred_element_type=F32)
        m1_rep = _top_values(s1)
        m2_rep = _top_values(s2)
        m1 = [m[0:1] for m in m1_rep]
        m2 = [m[0:1] for m in m2_rep]
        sub = lax.broadcasted_iota(jnp.int32, m1_rep[0].shape, 0)

        def stack(reps):
            out = reps[0]
            for r in range(1, hk):
                out = jnp.where(sub == r, reps[r], out)
            return out

        m2a = stack(m2_rep[:hk])
        m2b = stack(m2_rep[hk:])
        m1b = stack(m1_rep[hk:])
        cands = ([m1_rep[0] + m2a, m1_rep[0] + m2b] + [m1_rep[a] + m2a for a in range(1, hk)]
                 + [m1b + m2_rep[0]])
        top = m1[0] + m2[0]
        work = list(cands)
        theta = top
        for it in range(k):
            mx = functools.reduce(jnp.maximum, work)
            theta = jnp.max(mx, axis=0, keepdims=True)
            if it + 1 < k:
                work = [jnp.where(w == theta, -inf, w) for w in work]
        sel = [c >= theta for c in cands]
        z = functools.reduce(
            lambda a, b: a + b,
            [jnp.sum(jnp.where(sl, jnp.exp(c - top), 0.0), axis=0, keepdims=True)
             for sl, c in zip(sel, cands)])
        ones = [jnp.where(sl, 1.0, 0.0) for sl in sel]
        n_sel = [jnp.sum(ones[0] + ones[1], axis=0, keepdims=True)]
        n_sel += [jnp.sum(ones[a + 1], axis=0, keepdims=True) for a in range(1, hk)]
        n_sel_b = ones[hk + 1]
        cnt = jnp.zeros(s1.shape, F32)
        for a in range(hk):
            cnt = jnp.where(s1 == m1[a], n_sel[a], cnt)
        for r in range(k - hk):
            cnt = jnp.where(s1 == m1[hk + r], n_sel_b[r:r + 1], cnt)
        rank = jnp.full(s2.shape, float(k), F32)
        for b in reversed(range(k)):
            rank = jnp.where(s2 >= m2[b], float(b), rank)
        e1_ref[h] = jnp.exp(s1 - m1[0]) * (0.5 / z)
        cnt_ref[h] = cnt
        rank_ref[h] = rank.astype(rank_ref.dtype)
        e2_ref[h] = jnp.exp(s2 - m2[0]).astype(e2_ref.dtype)


def _route(x1t, wpqt, keys, tm):
    d, t = x1t.shape
    blk = pl.BlockSpec((PK_HEADS, N_KEYS, tm), lambda i: (0, 0, i))
    out = lambda dt: jax.ShapeDtypeStruct((PK_HEADS, N_KEYS, t), dt)
    return pl.pallas_call(
        _route_kernel,
        grid=(t // tm,),
        in_specs=[pl.BlockSpec((d, tm), lambda i: (0, i)), _resident(wpqt.shape),
                  _resident(keys.shape)],
        out_specs=[blk, blk, blk, blk],
        out_shape=[out(F32), out(F32), out(BF16), out(BF16)],
        name="peer_route",
        compiler_params=pltpu.CompilerParams(dimension_semantics=("parallel",),
                                             vmem_limit_bytes=VMEM_LIMIT_BYTES),
    )(x1t, wpqt, keys)


def _gate_chunk(key0, first_keys, cols, e1_ref, cnt_ref, rank_ref, e2_ref, h_rd, w_wr):
    inv_sqrt2 = 1.0 / math.sqrt(2.0)
    width = cols.stop - cols.start
    zero = jnp.zeros((), BF16)
    for ii in first_keys:
        rows = slice(ii * N_KEYS, (ii + 1) * N_KEYS)
        key = key0 + ii
        g = jnp.zeros((N_KEYS, width), BF16)
        for h in range(PK_HEADS):
            pack = (2 * SUBLANES, width)
            cnt_row = jnp.broadcast_to(cnt_ref[h, pl.ds(key, 1), cols], pack).astype(BF16)
            e1_row = jnp.broadcast_to(e1_ref[h, pl.ds(key, 1), cols], pack).astype(BF16)
            cnt_row = jnp.tile(cnt_row, (N_KEYS // (2 * SUBLANES), 1))
            e1_row = jnp.tile(e1_row, (N_KEYS // (2 * SUBLANES), 1))
            g = g + e1_row * jnp.where(rank_ref[h, :, cols] < cnt_row, e2_ref[h, :, cols], zero)
        hh = h_rd[rows, cols].astype(BF16)
        w_wr[rows, cols] = (hh * g) * (1.0 + lax.erf(hh * inv_sqrt2))


def _peer_kernel(x1t_ref, u_ref, vt_ref, e1_ref, cnt_ref, rank_ref, e2_ref, x1_ref, g2_ref, b2_ref,
                 o_ref, h0_sc, h1_sc, w_sc, acc_sc, *, te, tm, n_e, n_pairs, lane_chunk, alpha):
    g = pl.program_id(0)
    n_i = te // N_KEYS
    e_prev = jnp.clip(g - 1, 0, n_pairs - 1) % n_e
    key0 = e_prev * n_i
    first_keys = range(n_i)

    @pl.when(g == 0)
    def _():
        h1_sc[...] = jnp.zeros(h1_sc.shape, F32)

    @pl.when(e_prev == 0)
    def _():
        acc_sc[...] = jnp.zeros(acc_sc.shape, F32)

    def step(h_wr, h_rd):
        for c0 in range(0, tm, lane_chunk):
            cols = slice(c0, c0 + lane_chunk)
            h_wr[:, cols] = jnp.dot(u_ref[...], x1t_ref[:, cols], preferred_element_type=F32)
            _gate_chunk(key0, first_keys, cols, e1_ref, cnt_ref, rank_ref, e2_ref, h_rd, w_sc)
            acc_sc[:, cols] += lax.dot_general(vt_ref[...], w_sc[:, cols], (((0,), (0,)), ((), ())),
                                               preferred_element_type=F32)

    @pl.when(g % 2 == 0)
    def _():
        step(h0_sc, h1_sc)

    @pl.when(g % 2 == 1)
    def _():
        step(h1_sc, h0_sc)

    @pl.when(jnp.logical_and(e_prev == n_e - 1, g >= 1))
    def _():
        r = alpha * x1_ref[...] + acc_sc[...].T
        o_ref[...] = _layernorm(r, g2_ref[...], b2_ref[...])


def _peer(x1, x1t, u_bf, vt_bf, e1, cnt, rank, e2, ln_g, ln_b, tm, te, alpha):
    t, d = x1.shape
    n_e = u_bf.shape[0] // te
    n_pairs = (t // tm) * n_e
    pair = lambda g, lag: jnp.clip(g - lag, 0, n_pairs - 1)
    gate_blk = pl.BlockSpec((PK_HEADS, N_KEYS, tm), lambda g: (0, 0, pair(g, 1) // n_e))
    return pl.pallas_call(
        functools.partial(_peer_kernel, te=te, tm=tm, n_e=n_e, n_pairs=n_pairs,
                          lane_chunk=min(tm, 256), alpha=alpha),
        grid=(n_pairs + 1,),
        in_specs=[pl.BlockSpec((d, tm), lambda g: (0, pair(g, 0) // n_e)),
                  pl.BlockSpec((te, d), lambda g: (pair(g, 0) % n_e, 0)),
                  pl.BlockSpec((te, d), lambda g: (pair(g, 1) % n_e, 0)),
                  gate_blk, gate_blk, gate_blk, gate_blk,
                  pl.BlockSpec((tm, d), lambda g: (pair(g, 1) // n_e, 0)),
                  _resident(ln_g.shape), _resident(ln_b.shape)],
        out_specs=pl.BlockSpec((tm, d), lambda g: (pair(g, 1) // n_e, 0)),
        out_shape=jax.ShapeDtypeStruct((t, d), F32),
        scratch_shapes=[pltpu.VMEM((te, tm), F32), pltpu.VMEM((te, tm), F32),
                        pltpu.VMEM((te, tm), BF16), pltpu.VMEM((d, tm), F32)],
        name="peer_experts",
        compiler_params=pltpu.CompilerParams(dimension_semantics=("arbitrary",),
                                             vmem_limit_bytes=VMEM_LIMIT_BYTES),
    )(x1t, u_bf, vt_bf, e1, cnt, rank, e2, x1, ln_g, ln_b)


def _rope_freq_row():
    inv_freq = ROPE_THETA ** (-jnp.arange(0, ROT_DIM, 2, dtype=F32) / ROT_DIM)
    d = jnp.arange(V_DIM) % HEAD_DIM
    return jnp.where(d < ROT_DIM, inv_freq[d % (ROT_DIM // 2)], 0.0).astype(F32)[None, :]


def kernel(x, positions, w_in, conv_w, conv_b, gate_b, lambda_q1, lambda_k1, lambda_q2, lambda_k2,
           subln_g, w_conv_out, w_attn_out, w_o, ln1_g, ln1_b, w_pq, sub_keys, u_tab, v_tab,
           ln2_g, ln2_b):
    bsz, seq, d = x.shape
    depth = w_in.shape[0]
    t = bsz * seq
    assert d == N_HEADS * V_DIM and sub_keys.shape[1:] == (PK_HEADS, 2, N_KEYS, N_KEYS)
    tm = min(512, seq)
    tq = min(256, seq)
    te = 1024
    heads_per_step = 4
    alpha = (2 * depth) ** 0.25
    row = lambda a: a.reshape(1, -1).astype(F32)
    freq_row = _rope_freq_row()
    pos2 = positions.reshape(t, 1)
    x2 = x.reshape(t, d)
    for l in range(depth):
        lam_init = 0.8 - 0.6 * math.exp(-0.3 * l)
        w_l = w_in[l].astype(BF16)
        w_hbc, w_qkv, w_g = w_l[:, :3 * d], w_l[:, 3 * d:6 * d], w_l[:, 6 * d:]
        q, k, v = _qkv_rope(x2, pos2, freq_row, w_qkv, tm)
        y_attn = _attention(q, k, v, row(lambda_q1[l]), row(lambda_k1[l]), row(lambda_q2[l]),
                            row(lambda_k2[l]), row(subln_g[l]), bsz, seq, tq, heads_per_step,
                            lam_init)
        x1, x1t = _mix(x2, y_attn, w_hbc, w_g, w_conv_out[l].astype(BF16),
                       w_attn_out[l].astype(BF16), w_o[l].astype(BF16), conv_w[l].astype(F32),
                       row(conv_b[l]), row(gate_b[l]), row(ln1_g[l]), row(ln1_b[l]),
                       tm, seq, alpha)
        keys = sub_keys[l].reshape(PK_HEADS * 2, N_KEYS, N_KEYS).astype(BF16)
        e1, cnt, rank, e2 = _route(x1t, w_pq[l].T.astype(BF16), keys, min(256, tm))
        x2 = _peer(x1, x1t, u_tab[l].astype(BF16), v_tab[l].astype(BF16), e1, cnt, rank, e2,
                   row(ln2_g[l]), row(ln2_b[l]), tm, te, alpha)
    return x2.reshape(bsz, seq, d)
```

```python
import functools
import math

import jax
import jax.numpy as jnp
from jax import lax
from jax.experimental import pallas as pl
from jax.experimental.pallas import tpu as pltpu

N_HEADS = 8
HEAD_DIM = 64
V_DIM = 2 * HEAD_DIM
ROT_DIM = HEAD_DIM // 4
ROPE_THETA = 500000.0
CONV_K = 3
N_KEYS = 128
PK_HEADS = 8
PK_TOPK = 16
LN_EPS = 1e-5
CONV_HALO = 16

VMEM_LIMIT_BYTES = 52 * 1024 * 1024

F32 = jnp.float32
BF16 = jnp.bfloat16


def _resident(shape):
    nd = len(shape)
    return pl.BlockSpec(shape, lambda *_: (0,) * nd, pipeline_mode=pl.Buffered(1))


def _layernorm(r, g, b):
    mu = jnp.mean(r, axis=-1, keepdims=True)
    c = r - mu
    var = jnp.mean(c * c, axis=-1, keepdims=True)
    return c * lax.rsqrt(var + LN_EPS) * g + b


def _qkv_rope_kernel(x_ref, pos_ref, freq_ref, w_ref, q_ref, k_ref, v_ref):
    d_model = x_ref.shape[1]
    proj = jnp.dot(x_ref[...].astype(BF16), w_ref[...], preferred_element_type=F32)
    ang = pos_ref[...].astype(F32) * freq_ref[...]
    cos = jnp.cos(ang)
    sin = jnp.sin(ang)
    d = lax.broadcasted_iota(jnp.int32, ang.shape, 1) & (HEAD_DIM - 1)
    half = ROT_DIM // 2
    lo = d < half
    hi = jnp.logical_and(d >= half, d < ROT_DIM)
    c_tab = jnp.where(d < ROT_DIM, cos, 1.0)
    s_lo = jnp.where(lo, -sin, 0.0)
    s_hi = jnp.where(hi, sin, 0.0)
    scale = HEAD_DIM ** -0.5 * math.log2(math.e)
    for h in range(N_HEADS):
        for off, ref, mul in ((0, q_ref, scale), (d_model, k_ref, 1.0)):
            t = proj[:, off + h * V_DIM: off + (h + 1) * V_DIM]
            up = jnp.where(lo, pltpu.roll(t, V_DIM - half, 1), 0.0)
            dn = jnp.where(hi, pltpu.roll(t, half, 1), 0.0)
            r = t * c_tab + up * s_lo + dn * s_hi
            ref[:, h * V_DIM:(h + 1) * V_DIM] = (r * mul).astype(BF16)
    v_ref[...] = proj[:, 2 * d_model:].astype(BF16)


def _qkv_rope(x2, pos2, freq_row, w_qkv, tm):
    t, d = x2.shape
    out = jax.ShapeDtypeStruct((t, d), BF16)
    tok = pl.BlockSpec((tm, d), lambda i: (i, 0))
    return pl.pallas_call(
        _qkv_rope_kernel,
        grid=(t // tm,),
        in_specs=[tok, pl.BlockSpec((tm, 1), lambda i: (i, 0)), _resident(freq_row.shape),
                  _resident(w_qkv.shape)],
        out_specs=[tok, tok, tok],
        out_shape=[out, out, out],
        name="qkv_rope",
        compiler_params=pltpu.CompilerParams(dimension_semantics=("parallel",),
                                             vmem_limit_bytes=VMEM_LIMIT_BYTES),
    )(x2, pos2, freq_row, w_qkv)


def _attn_tile(q_ref, k_ref, v_ref, g_ref, o_ref, lam, *, n, tq, heads, lam_init):
    n_tiles = n // V_DIM
    diag0 = (n - tq) // V_DIM
    row = lax.broadcasted_iota(jnp.int32, (2 * tq, V_DIM), 0)
    qpos = jnp.where(row >= tq, row - tq, row)
    lane = lax.broadcasted_iota(jnp.int32, (2 * tq, V_DIM), 1)
    lane_q = lax.broadcasted_iota(jnp.int32, (tq, V_DIM), 1)
    cols = [slice(h * V_DIM, (h + 1) * V_DIM) for h in range(heads)]
    scores = []
    for c in cols:
        q = q_ref[:, c]
        zero = jnp.zeros_like(q)
        qq = jnp.concatenate([jnp.where(lane_q < HEAD_DIM, q, zero),
                              jnp.where(lane_q >= HEAD_DIM, q, zero)], axis=0)
        scores.append(lax.dot_general(qq, k_ref[0:n, c], (((1,), (1,)), ((), ())),
                                      preferred_element_type=F32))
    tiles = []
    for s in scores:
        ts = []
        for j in range(n_tiles):
            t = s[:, j * V_DIM:(j + 1) * V_DIM]
            if j >= diag0:
                t = jnp.where(lane + (j - diag0) * V_DIM <= qpos, t, -jnp.inf)
            ts.append(t)
        tiles.append(ts)
    mx = [jnp.max(functools.reduce(jnp.maximum, ts), axis=-1, keepdims=True) for ts in tiles]
    p = [[jnp.exp2(t - m) for t in ts] for ts, m in zip(tiles, mx)]
    r = [1.0 / jnp.sum(functools.reduce(lambda a, b: a + b, ps), axis=-1, keepdims=True)
         for ps in p]
    outs = []
    for ps, rh, c in zip(p, r, cols):
        r1 = rh[:tq]
        r2 = lam * rh[tq:]
        a = jnp.concatenate([(pt[:tq] * r1 - pt[tq:] * r2).astype(BF16) for pt in ps], axis=1)
        outs.append(jnp.dot(a, v_ref[0:n, c], preferred_element_type=F32))
    for o, c in zip(outs, cols):
        ms = jnp.mean(o * o, axis=-1, keepdims=True)
        y = o * lax.rsqrt(ms + LN_EPS) * g_ref[...] * (1.0 - lam_init)
        o_ref[:, c] = y.astype(o_ref.dtype)


def _attn_kernel(lq1_ref, lk1_ref, lq2_ref, lk2_ref, g_ref, q_ref, k_ref, v_ref, o_ref,
                 *, tq, nq, heads, lam_init):
    i = pl.program_id(2)
    lam = (jnp.exp(jnp.sum(lq1_ref[...] * lk1_ref[...], axis=-1, keepdims=True))
           - jnp.exp(jnp.sum(lq2_ref[...] * lk2_ref[...], axis=-1, keepdims=True)) + lam_init)
    for nb in range(1, nq + 1):
        @pl.when(i == nb - 1)
        def _(nb=nb):
            _attn_tile(q_ref, k_ref, v_ref, g_ref, o_ref, lam, n=nb * tq, tq=tq, heads=heads,
                       lam_init=lam_init)


def _attention(q, k, v, lq1, lk1, lq2, lk2, subln_g, bsz, seq, tq, heads, lam_init):
    t, d = q.shape
    nq = seq // tq
    width = heads * V_DIM
    vec = lambda a: _resident(a.shape)
    return pl.pallas_call(
        functools.partial(_attn_kernel, tq=tq, nq=nq, heads=heads, lam_init=lam_init),
        grid=(bsz, N_HEADS // heads, nq),
        in_specs=[vec(lq1), vec(lk1), vec(lq2), vec(lk2), vec(subln_g),
                  pl.BlockSpec((tq, width), lambda b, h, i: (b * nq + i, h)),
                  pl.BlockSpec((seq, width), lambda b, h, i: (b, h)),
                  pl.BlockSpec((seq, width), lambda b, h, i: (b, h))],
        out_specs=pl.BlockSpec((tq, width), lambda b, h, i: (b * nq + i, h)),
        out_shape=jax.ShapeDtypeStruct((t, d), BF16),
        name="diff_attn",
        compiler_params=pltpu.CompilerParams(
            dimension_semantics=("parallel", "parallel", "arbitrary"),
            vmem_limit_bytes=VMEM_LIMIT_BYTES),
    )(lq1, lk1, lq2, lk2, subln_g, q, k, v)


def _mix_kernel(xp_ref, x_ref, ya_ref, w_hbc_ref, w_g_ref, w_co_ref, w_ao_ref, w_o_ref,
                cw_ref, cb_ref, gb_ref, g1_ref, b1_ref, x1_ref, x1t_ref, z_sc,
                *, tm, tiles_per_seq, alpha):
    i = pl.program_id(0)
    d = x_ref.shape[1]
    x = x_ref[...]
    xcat = jnp.concatenate([xp_ref[...], x], axis=0).astype(BF16)
    hbc = jnp.dot(xcat, w_hbc_ref[...], preferred_element_type=F32)
    z = hbc[:, :d] * hbc[:, 2 * d:]
    row = lax.broadcasted_iota(jnp.int32, z.shape, 0)
    seq_start = (i % tiles_per_seq) == 0
    z_sc[...] = jnp.where(jnp.logical_and(row < CONV_HALO, seq_start), 0.0, z)
    cw = cw_ref[...]
    y = cb_ref[...] + cw[CONV_K - 1:CONV_K] * z_sc[pl.ds(CONV_HALO, tm), :]
    for j in range(CONV_K - 1):
        y = y + cw[j:j + 1] * z_sc[pl.ds(CONV_HALO - (CONV_K - 1) + j, tm), :]
    y_conv = hbc[CONV_HALO:, d:2 * d] * y
    c_out = jnp.dot(y_conv.astype(BF16), w_co_ref[...], preferred_element_type=F32)
    a_out = jnp.dot(ya_ref[...], w_ao_ref[...], preferred_element_type=F32)
    gates = jax.nn.sigmoid(
        jnp.dot(xcat[CONV_HALO:], w_g_ref[...], preferred_element_type=F32) + gb_ref[...])
    merged = gates[:, :d] * c_out + gates[:, d:] * a_out
    r = alpha * x + jnp.dot(merged.astype(BF16), w_o_ref[...], preferred_element_type=F32)
    x1 = _layernorm(r, g1_ref[...], b1_ref[...])
    x1_ref[...] = x1
    x1t_ref[...] = x1.T.astype(BF16)


def _mix(x2, y_attn, w_hbc, w_g, w_co, w_ao, w_o, conv_w, conv_b, gate_b, ln_g, ln_b,
         tm, seq, alpha):
    t, d = x2.shape
    halo_blocks = tm // CONV_HALO
    tok = pl.BlockSpec((tm, d), lambda i: (i, 0))
    prev = pl.BlockSpec((CONV_HALO, d), lambda i: (jnp.maximum(i * halo_blocks - 1, 0), 0))
    res = lambda a: _resident(a.shape)
    return pl.pallas_call(
        functools.partial(_mix_kernel, tm=tm, tiles_per_seq=seq // tm, alpha=alpha),
        grid=(t // tm,),
        in_specs=[prev, tok, tok, res(w_hbc), res(w_g), res(w_co), res(w_ao), res(w_o),
                  res(conv_w), res(conv_b), res(gate_b), res(ln_g), res(ln_b)],
        out_specs=[tok, pl.BlockSpec((d, tm), lambda i: (0, i))],
        out_shape=[jax.ShapeDtypeStruct((t, d), F32), jax.ShapeDtypeStruct((d, t), BF16)],
        scratch_shapes=[pltpu.VMEM((CONV_HALO + tm, d), F32)],
        name="conv_merge_ln1",
        compiler_params=pltpu.CompilerParams(dimension_semantics=("parallel",),
                                             vmem_limit_bytes=VMEM_LIMIT_BYTES),
    )(x2, x2, y_attn, w_hbc, w_g, w_co, w_ao, w_o, conv_w, conv_b, gate_b, ln_g, ln_b)


def _sort_network(n):
    pairs = []

    def merge(lo, hi, r):
        step = r * 2
        if step < hi - lo:
            merge(lo, hi, step)
            merge(lo + r, hi, step)
            pairs.extend((i, i + r) for i in range(lo + r, hi - r, step))
        else:
            pairs.append((lo, lo + r))

    def sort(lo, hi):
        if hi - lo >= 1:
            mid = lo + (hi - lo) // 2
            sort(lo, mid)
            sort(mid + 1, hi)
            merge(lo, hi, 1)

    sort(0, n - 1)
    return tuple(pairs)


SUBLANES = 8
_SORT_PAIRS = _sort_network(PK_TOPK)
_BITONIC_PAIRS = tuple((i, i + d) for d in (8, 4, 2, 1) for i in range(PK_TOPK) if not i & d)


def _compare_exchange(a, pairs):
    for i, j in pairs:
        a[i], a[j] = jnp.maximum(a[i], a[j]), jnp.minimum(a[i], a[j])


def _top_values(s):
    k = PK_TOPK
    assert s.shape[0] == k * SUBLANES
    a = [s[SUBLANES * v:SUBLANES * (v + 1)] for v in range(k)]
    _compare_exchange(a, _SORT_PAIRS)
    shift = SUBLANES // 2
    while shift:
        b = [pltpu.roll(x, shift, 0) for x in a]
        a = [jnp.maximum(a[i], b[k - 1 - i]) for i in range(k)]
        _compare_exchange(a, _BITONIC_PAIRS)
        shift //= 2
    return a


def _route_kernel(x1t_ref, wpqt_ref, keys_ref, e1_ref, cnt_ref, rank_ref, e2_ref):
    qt = jnp.dot(wpqt_ref[...], x1t_ref[...], preferred_element_type=F32).astype(BF16)
    inf = jnp.inf
    k = PK_TOPK
    hk = k // 2
    for h in range(PK_HEADS):
        s1 = jnp.dot(keys_ref[2 * h], qt[(2 * h) * N_KEYS:(2 * h + 1) * N_KEYS],
                     preferred_element_type=F32)
        s2 = jnp.dot(keys_ref[2 * h + 1], qt[(2 * h + 1) * N_KEYS:(2 * h + 2) * N_KEYS],
                     preferred_element_type=F32)
        m1_rep = _top_values(s1)
        m2_rep = _top_values(s2)
        m1 = [m[0:1] for m in m1_rep]
        m2 = [m[0:1] for m in m2_rep]
        sub = lax.broadcasted_iota(jnp.int32, m1_rep[0].shape, 0)

        def stack(reps):
            out = reps[0]
            for r in range(1, hk):
                out = jnp.where(sub == r, reps[r], out)
            return out

        m2a = stack(m2_rep[:hk])
        m2b = stack(m2_rep[hk:])
        m1b = stack(m1_rep[hk:])
        cands = ([m1_rep[0] + m2a, m1_rep[0] + m2b] + [m1_rep[a] + m2a for a in range(1, hk)]
                 + [m1b + m2_rep[0]])
        top = m1[0] + m2[0]
        work = list(cands)
        theta = top
        for it in range(k):
            mx = functools.reduce(jnp.maximum, work)
            theta = jnp.max(mx, axis=0, keepdims=True)
            if it + 1 < k:
                work = [jnp.where(w == theta, -inf, w) for w in work]
        sel = [c >= theta for c in cands]
        z = functools.reduce(
            lambda a, b: a + b,
            [jnp.sum(jnp.where(sl, jnp.exp(c - top), 0.0), axis=0, keepdims=True)
             for sl, c in zip(sel, cands)])
        ones = [jnp.where(sl, 1.0, 0.0) for sl in sel]
        n_sel = [jnp.sum(ones[0] + ones[1], axis=0, keepdims=True)]
        n_sel += [jnp.sum(ones[a + 1], axis=0, keepdims=True) for a in range(1, hk)]
        n_sel_b = ones[hk + 1]
        cnt = jnp.zeros(s1.shape, F32)
        for a in range(hk):
            cnt = jnp.where(s1 == m1[a], n_sel[a], cnt)
        for r in range(k - hk):
            cnt = jnp.where(s1 == m1[hk + r], n_sel_b[r:r + 1], cnt)
        rank = jnp.full(s2.shape, float(k), F32)
        for b in reversed(range(k)):
            rank = jnp.where(s2 >= m2[b], float(b), rank)
        e1_ref[h] = jnp.exp(s1 - m1[0]) * (0.5 / z)
        cnt_ref[h] = cnt
        rank_ref[h] = rank.astype(rank_ref.dtype)
        e2_ref[h] = jnp.exp(s2 - m2[0]).astype(e2_ref.dtype)


def _route(x1t, wpqt, keys, tm):
    d, t = x1t.shape
    blk = pl.BlockSpec((PK_HEADS, N_KEYS, tm), lambda i: (0, 0, i))
    out = lambda dt: jax.ShapeDtypeStruct((PK_HEADS, N_KEYS, t), dt)
    return pl.pallas_call(
        _route_kernel,
        grid=(t // tm,),
        in_specs=[pl.BlockSpec((d, tm), lambda i: (0, i)), _resident(wpqt.shape),
                  _resident(keys.shape)],
        out_specs=[blk, blk, blk, blk],
        out_shape=[out(F32), out(F32), out(BF16), out(BF16)],
        name="peer_route",
        compiler_params=pltpu.CompilerParams(dimension_semantics=("parallel",),
                                             vmem_limit_bytes=VMEM_LIMIT_BYTES),
    )(x1t, wpqt, keys)


def _gate_chunk(key0, first_keys, cols, e1_ref, cnt_ref, rank_ref, e2_ref, h_rd, w_wr):
    inv_sqrt2 = 1.0 / math.sqrt(2.0)
    width = cols.stop - cols.start
    zero = jnp.zeros((), BF16)
    for ii in first_keys:
        rows = slice(ii * N_KEYS, (ii + 1) * N_KEYS)
        key = key0 + ii
        g = jnp.zeros((N_KEYS, width), BF16)
        for h in range(PK_HEADS):
            pack = (2 * SUBLANES, width)
            cnt_row = jnp.broadcast_to(cnt_ref[h, pl.ds(key, 1), cols], pack).astype(BF16)
            e1_row = jnp.broadcast_to(e1_ref[h, pl.ds(key, 1), cols], pack).astype(BF16)
            cnt_row = jnp.tile(cnt_row, (N_KEYS // (2 * SUBLANES), 1))
            e1_row = jnp.tile(e1_row, (N_KEYS // (2 * SUBLANES), 1))
            g = g + e1_row * jnp.where(rank_ref[h, :, cols] < cnt_row, e2_ref[h, :, cols], zero)
        hh = h_rd[rows, cols].astype(BF16)
        w_wr[rows, cols] = (hh * g) * (1.0 + lax.erf(hh * inv_sqrt2))


def _peer_kernel(x1t_ref, u_ref, vt_ref, e1_ref, cnt_ref, rank_ref, e2_ref, x1_ref, g2_ref, b2_ref,
                 o_ref, h0_sc, h1_sc, w_sc, acc_sc, *, te, tm, n_e, n_pairs, lane_chunk, alpha):
    g = pl.program_id(0)
    n_i = te // N_KEYS
    e_prev = jnp.clip(g - 1, 0, n_pairs - 1) % n_e
    key0 = e_prev * n_i
    first_keys = range(n_i)

    @pl.when(g == 0)
    def _():
        h1_sc[...] = jnp.zeros(h1_sc.shape, F32)

    @pl.when(e_prev == 0)
    def _():
        acc_sc[...] = jnp.zeros(acc_sc.shape, F32)

    def step(h_wr, h_rd):
        for c0 in range(0, tm, lane_chunk):
            cols = slice(c0, c0 + lane_chunk)
            h_wr[:, cols] = jnp.dot(u_ref[...], x1t_ref[:, cols], preferred_element_type=F32)
            _gate_chunk(key0, first_keys, cols, e1_ref, cnt_ref, rank_ref, e2_ref, h_rd, w_sc)
            acc_sc[:, cols] += lax.dot_general(vt_ref[...], w_sc[:, cols], (((0,), (0,)), ((), ())),
                                               preferred_element_type=F32)

    @pl.when(g % 2 == 0)
    def _():
        step(h0_sc, h1_sc)

    @pl.when(g % 2 == 1)
    def _():
        step(h1_sc, h0_sc)

    @pl.when(jnp.logical_and(e_prev == n_e - 1, g >= 1))
    def _():
        r = alpha * x1_ref[...] + acc_sc[...].T
        o_ref[...] = _layernorm(r, g2_ref[...], b2_ref[...])


def _peer(x1, x1t, u_bf, vt_bf, e1, cnt, rank, e2, ln_g, ln_b, tm, te, alpha):
    t, d = x1.shape
    n_e = u_bf.shape[0] // te
    n_pairs = (t // tm) * n_e
    pair = lambda g, lag: jnp.clip(g - lag, 0, n_pairs - 1)
    gate_blk = pl.BlockSpec((PK_HEADS, N_KEYS, tm), lambda g: (0, 0, pair(g, 1) // n_e))
    return pl.pallas_call(
        functools.partial(_peer_kernel, te=te, tm=tm, n_e=n_e, n_pairs=n_pairs,
                          lane_chunk=min(tm, 256), alpha=alpha),
        grid=(n_pairs + 1,),
        in_specs=[pl.BlockSpec((d, tm), lambda g: (0, pair(g, 0) // n_e)),
                  pl.BlockSpec((te, d), lambda g: (pair(g, 0) % n_e, 0)),
                  pl.BlockSpec((te, d), lambda g: (pair(g, 1) % n_e, 0)),
                  gate_blk, gate_blk, gate_blk, gate_blk,
                  pl.BlockSpec((tm, d), lambda g: (pair(g, 1) // n_e, 0)),
                  _resident(ln_g.shape), _resident(ln_b.shape)],
        out_specs=pl.BlockSpec((tm, d), lambda g: (pair(g, 1) // n_e, 0)),
        out_shape=jax.ShapeDtypeStruct((t, d), F32),
        scratch_shapes=[pltpu.VMEM((te, tm), F32), pltpu.VMEM((te, tm), F32),
                        pltpu.VMEM((te, tm), BF16), pltpu.VMEM((d, tm), F32)],
        name="peer_experts",
        compiler_params=pltpu.CompilerParams(dimension_semantics=("arbitrary",),
                                             vmem_limit_bytes=VMEM_LIMIT_BYTES),
    )(x1t, u_bf, vt_bf, e1, cnt, rank, e2, x1, ln_g, ln_b)


def _rope_freq_row():
    inv_freq = ROPE_THETA ** (-jnp.arange(0, ROT_DIM, 2, dtype=F32) / ROT_DIM)
    d = jnp.arange(V_DIM) % HEAD_DIM
    return jnp.where(d < ROT_DIM, inv_freq[d % (ROT_DIM // 2)], 0.0).astype(F32)[None, :]


def kernel(x, positions, w_in, conv_w, conv_b, gate_b, lambda_q1, lambda_k1, lambda_q2, lambda_k2,
           subln_g, w_conv_out, w_attn_out, w_o, ln1_g, ln1_b, w_pq, sub_keys, u_tab, v_tab,
           ln2_g, ln2_b):
    bsz, seq, d = x.shape
    depth = w_in.shape[0]
    t = bsz * seq
    assert d == N_HEADS * V_DIM and sub_keys.shape[1:] == (PK_HEADS, 2, N_KEYS, N_KEYS)
    tm = min(512, seq)
    tq = min(256, seq)
    te = 2048
    heads_per_step = 4
    alpha = (2 * depth) ** 0.25
    row = lambda a: a.reshape(1, -1).astype(F32)
    freq_row = _rope_freq_row()
    pos2 = positions.reshape(t, 1)
    x2 = x.reshape(t, d)
    for l in range(depth):
        lam_init = 0.8 - 0.6 * math.exp(-0.3 * l)
        w_l = w_in[l].astype(BF16)
        w_hbc, w_qkv, w_g = w_l[:, :3 * d], w_l[:, 3 * d:6 * d], w_l[:, 6 * d:]
        q, k, v = _qkv_rope(x2, pos2, freq_row, w_qkv, tm)
        y_attn = _attention(q, k, v, row(lambda_q1[l]), row(lambda_k1[l]), row(lambda_q2[l]),
                            row(lambda_k2[l]), row(subln_g[l]), bsz, seq, tq, heads_per_step,
                            lam_init)
        x1, x1t = _mix(x2, y_attn, w_hbc, w_g, w_conv_out[l].astype(BF16),
                       w_attn_out[l].astype(BF16), w_o[l].astype(BF16), conv_w[l].astype(F32),
                       row(conv_b[l]), row(gate_b[l]), row(ln1_g[l]), row(ln1_b[l]),
                       tm, seq, alpha)
        keys = sub_keys[l].reshape(PK_HEADS * 2, N_KEYS, N_KEYS).astype(BF16)
        e1, cnt, rank, e2 = _route(x1t, w_pq[l].T.astype(BF16), keys, min(256, tm))
        x2 = _peer(x1, x1t, u_tab[l].astype(BF16), v_tab[l].astype(BF16), e1, cnt, rank, e2,
                   row(ln2_g[l]), row(ln2_b[l]), tm, te, alpha)
    return x2.reshape(bsz, seq, d)
```
